```python
import math
import jax
import jax.numpy as jnp
from jax import lax
import numpy as np

D_MODEL = 2048
BATCH = 2
SEQ = 8192
DEPTH = 1
DEC_BATCH = 32
DEC_SEQ = 1
PAST_LEN = 16384
PAGE_SIZE = 128

N_HEADS = 8
N_KV_HEADS = 4
Q_PER_KV = N_HEADS // N_KV_HEADS
HEAD_DIM = D_MODEL // (2 * N_HEADS)
ATTN_WIDTH = N_HEADS * 2 * HEAD_DIM
KV_WIDTH = N_KV_HEADS * 2 * HEAD_DIM
Q_BLOCK = 128
N_BUCKETS = 32
MAX_DISTANCE = 128
D_INNER = 2 * D_MODEL
SSM_HEAD_DIM = 64
SSM_HEADS = D_INNER // SSM_HEAD_DIM
SSM_GROUPS = 8
SSM_HPG = SSM_HEADS // SSM_GROUPS
D_STATE = 128
CONV_WIDTH = 4
CONV_DIM = D_INNER + 2 * SSM_GROUPS * D_STATE
SSD_CHUNK = 128
MOE_GROUPS = 4
EXPERTS_PER_GROUP = 8
N_EXPERTS = MOE_GROUPS * EXPERTS_PER_GROUP
TOP_K_IN_GROUP = 2
D_FF_EXPERT = D_MODEL // 2
MOE_BLOCK = 128
ALPHA = (2 * DEPTH) ** 0.25
BETA = (8 * DEPTH) ** -0.25
EPS = 1e-5
IN_SIZES = (ATTN_WIDTH, KV_WIDTH, KV_WIDTH, D_INNER, CONV_DIM, SSM_HEADS, D_MODEL, D_MODEL)

kernel_name = "diffattn_mamba2_gated_hmoe_deepnorm_step"


def layer_norm(x, g, b):
    xf = x.astype(jnp.float32)
    xc = xf - xf.mean(-1, keepdims=True)
    var = (xc * xc).mean(-1, keepdims=True)
    return (xc * lax.rsqrt(var + EPS) * g + b).astype(x.dtype)


def rms_norm(x, w):
    xf = x.astype(jnp.float32)
    return (xf * lax.rsqrt((xf * xf).mean(-1, keepdims=True) + EPS) * w).astype(x.dtype)


def split_projection(h, w_in):
    proj = h @ w_in
    parts, off = [], 0
    for size in IN_SIZES:
        parts.append(proj[..., off:off + size])
        off += size
    return parts


def t5_bucket(dist):
    n = jnp.maximum(dist, 0)
    max_exact = N_BUCKETS // 2
    nf = jnp.maximum(n, 1).astype(jnp.float32)
    large = max_exact + (jnp.log(nf / max_exact) / math.log(MAX_DISTANCE / max_exact)
                         * (N_BUCKETS - max_exact)).astype(jnp.int32)
    large = jnp.minimum(large, N_BUCKETS - 1)
    return jnp.where(n < max_exact, n, large)


def diff_attention(q, k, v, q_pos, k_pos, rel_bias, lam, lam_init, subln_w):
    b, lq = q.shape[:2]
    lk = k.shape[1]
    qg = q.reshape(b, lq, N_KV_HEADS, Q_PER_KV, 2, HEAD_DIM)
    logits = jnp.einsum('bqgrmd,bkgmd->bgrmqk', qg, k, preferred_element_type=jnp.float32)
    dist = q_pos[:, None] - k_pos[None, :]
    bias = rel_bias.astype(jnp.float32)[t5_bucket(dist)]
    bias = jnp.transpose(bias, (2, 0, 1)).reshape(N_KV_HEADS, Q_PER_KV, 1, lq, lk)
    logits = jnp.where(dist >= 0, logits + bias, -jnp.inf)
    p = jax.nn.softmax(logits, axis=-1)
    attn = p[:, :, :, 0] - lam * p[:, :, :, 1]
    o = jnp.einsum('bgrqk,bkgd->bqgrd', attn.astype(v.dtype), v)
    o = rms_norm(o.reshape(b, lq, N_HEADS, 2 * HEAD_DIM), subln_w) * (1.0 - lam_init)
    return o.reshape(b, lq, ATTN_WIDTH)


def ssd_scan(x, a, b_in, c_in, h0, chunk):
    bsz, length = x.shape[:2]
    nc = length // chunk

    def to_chunks(t):
        return jnp.moveaxis(t.reshape((bsz, nc, chunk) + t.shape[2:]), 1, 0)

    xc = to_chunks(x.reshape(bsz, length, SSM_GROUPS, SSM_HPG, SSM_HEAD_DIM))
    ac = to_chunks(a.reshape(bsz, length, SSM_GROUPS, SSM_HPG))
    bc = to_chunks(b_in)
    cc = to_chunks(c_in)
    causal = jnp.tril(jnp.ones((chunk, chunk), bool))

    def step(h, inp):
        x_k, a_k, b_k, c_k = inp
        acs = jnp.cumsum(a_k, axis=1)
        seg = acs[:, :, None] - acs[:, None, :]
        decay_ls = jnp.exp(jnp.where(causal[None, :, :, None, None], seg, -jnp.inf))
        cb = jnp.einsum('blgn,bsgn->blsg', c_k, b_k)
        y_diag = jnp.einsum('blsg,blsgr,bsgrp->blgrp', cb, decay_ls, x_k)
        y_off = jnp.einsum('blgn,bgrpn->blgrp', c_k, h) * jnp.exp(acs)[..., None]
        decay_end = jnp.exp(acs[:, -1:] - acs)
        h_new = (h * jnp.exp(acs[:, -1])[..., None, None]
                 + jnp.einsum('bsgn,bsgr,bsgrp->bgrpn', b_k, decay_end, x_k))
        return h_new, y_diag + y_off

    h_init = h0.astype(jnp.float32).reshape(bsz, SSM_GROUPS, SSM_HPG, SSM_HEAD_DIM, D_STATE)
    h_final, ys = lax.scan(step, h_init, (xc, ac, bc, cc))
    y = jnp.moveaxis(ys, 0, 1).reshape(bsz, length, SSM_HEADS, SSM_HEAD_DIM)
    return y, h_final.reshape(bsz, SSM_HEADS, SSM_HEAD_DIM, D_STATE)


def mamba_mixer(z, xbc, dt_raw, conv_prev, ssm_prev, chunk, conv_w, conv_b, dt_bias, a_log, d_skip, norm_w):
    bsz, length = z.shape[:2]
    ext = jnp.concatenate([conv_prev.astype(xbc.dtype), xbc], axis=1)
    conv = lax.conv_general_dilated(ext, conv_w[:, None, :].astype(ext.dtype), window_strides=(1,),
                                    padding='VALID', dimension_numbers=('NWC', 'WIO', 'NWC'),
                                    feature_group_count=CONV_DIM)
    new_conv = ext[:, -(CONV_WIDTH - 1):]
    act = jax.nn.silu(conv + conv_b).astype(jnp.float32)
    gn = SSM_GROUPS * D_STATE
    xs = act[..., :D_INNER].reshape(bsz, length, SSM_HEADS, SSM_HEAD_DIM)
    bs = act[..., D_INNER:D_INNER + gn].reshape(bsz, length, SSM_GROUPS, D_STATE)
    cs = act[..., D_INNER + gn:].reshape(bsz, length, SSM_GROUPS, D_STATE)
    dt = jax.nn.softplus(dt_raw.astype(jnp.float32) + dt_bias.astype(jnp.float32))
    a = -jnp.exp(a_log.astype(jnp.float32))
    y, h_new = ssd_scan(xs * dt[..., None], dt * a, bs, cs, ssm_prev, chunk)
    y = y + d_skip.astype(jnp.float32)[:, None] * xs
    y = y.reshape(bsz, length, D_INNER) * jax.nn.silu(z.astype(jnp.float32))
    yg = y.reshape(bsz, length, SSM_GROUPS, D_INNER // SSM_GROUPS)
    yg = yg * lax.rsqrt((yg * yg).mean(-1, keepdims=True) + EPS)
    y = yg.reshape(bsz, length, D_INNER) * norm_w
    return y.astype(z.dtype), new_conv, h_new


def grouped_experts(x, expert_ids, weights, w_gate, w_up, w_down):
    t, kk = expert_ids.shape
    n_assign = t * kk
    flat = expert_ids.reshape(-1)
    order = jnp.argsort(flat)
    sorted_e = flat[order]
    counts = jnp.bincount(flat, length=N_EXPERTS)
    starts = jnp.cumsum(counts) - counts
    padded = (counts + MOE_BLOCK - 1) // MOE_BLOCK * MOE_BLOCK
    pad_ends = jnp.cumsum(padded)
    pad_starts = pad_ends - padded
    dest_sorted = pad_starts[sorted_e] + jnp.arange(n_assign) - starts[sorted_e]
    n_blocks = -(-(n_assign + N_EXPERTS * (MOE_BLOCK - 1)) // MOE_BLOCK)
    rows = n_blocks * MOE_BLOCK
    row_token = jnp.full((rows,), t, jnp.int32).at[dest_sorted].set((order // kk).astype(jnp.int32))
    block_expert = jnp.minimum(
        jnp.searchsorted(pad_ends, jnp.arange(n_blocks) * MOE_BLOCK, side='right'), N_EXPERTS - 1)
    x_pad = jnp.concatenate([x, jnp.zeros((1, x.shape[1]), x.dtype)], axis=0)
    xb = x_pad[row_token].reshape(n_blocks, MOE_BLOCK, x.shape[1])

    def expert_block(args):
        xb_i, e = args
        return (jax.nn.silu(xb_i @ w_gate[e]) * (xb_i @ w_up[e])) @ w_down[e]

    yb = lax.map(expert_block, (xb, block_expert)).reshape(rows, x.shape[1])
    dest = jnp.zeros((n_assign,), jnp.int32).at[order].set(dest_sorted.astype(jnp.int32))
    y = yb[dest].reshape(t, kk, x.shape[1])
    return jnp.einsum('tk,tkd->td', weights.astype(y.dtype), y)


def hierarchical_moe(h, w_route_group, b_route_group, w_route_expert, b_route_expert, w_gate, w_up, w_down):
    shape = h.shape
    x = h.reshape(-1, D_MODEL)
    t = x.shape[0]
    g_logits = (x @ w_route_group).astype(jnp.float32) + b_route_group.astype(jnp.float32)
    g_prob = jax.nn.softmax(g_logits, axis=-1)
    _, g_idx = lax.top_k(g_logits, 1)
    g_w = jnp.take_along_axis(g_prob, g_idx, axis=1)
    e_logits = ((x @ w_route_expert).astype(jnp.float32) + b_route_expert.astype(jnp.float32)
                ).reshape(t, MOE_GROUPS, EXPERTS_PER_GROUP)
    e_logits = jnp.take_along_axis(e_logits, g_idx[:, :, None], axis=1)[:, 0]
    e_top, e_idx = lax.top_k(e_logits, TOP_K_IN_GROUP)
    weights = g_w * jax.nn.softmax(e_top, axis=-1)
    expert_ids = g_idx * EXPERTS_PER_GROUP + e_idx
    return grouped_experts(x, expert_ids, weights, w_gate, w_up, w_down).reshape(shape)


def layer_tail(h, attn_o, ssm_o, gate_a, gate_b, w_branch_attn, w_branch_ssm, w_out, ln1_g, ln1_b,
               w_route_group, b_route_group, w_route_expert, b_route_expert, w_gate, w_up, w_down,
               ln2_g, ln2_b):
    merged = (jax.nn.sigmoid(gate_a) * (attn_o @ w_branch_attn)
              + jax.nn.sigmoid(gate_b) * (ssm_o @ w_branch_ssm))
    h1 = layer_norm(ALPHA * h + merged @ w_out, ln1_g, ln1_b)
    ffn = hierarchical_moe(h1, w_route_group, b_route_group, w_route_expert, b_route_expert,
                           w_gate, w_up, w_down)
    return layer_norm(ALPHA * h1 + ffn, ln2_g, ln2_b)


def setup_inputs(seed: int = 0) -> dict:
    key = jax.random.key(seed)
    ks = jax.random.split(key, 40)
    f32 = jnp.float32

    def nrm(i, shape, scale):
        return jax.random.normal(ks[i], shape, f32) * scale

    n_pages = PAST_LEN // PAGE_SIZE
    n_used = DEC_BATCH * n_pages
    n_pool = n_used + -(-n_used // 4)
    page_table = jax.random.permutation(ks[0], n_pool)[:n_used].reshape(DEC_BATCH, n_pages).astype(jnp.int32)
    dt0 = jnp.exp(jax.random.uniform(ks[1], (SSM_HEADS,), f32, math.log(1e-3), math.log(1e-1)))
    dt_bias = dt0 + jnp.log(-jnp.expm1(-dt0))
    a_log = jnp.log(jax.random.uniform(ks[2], (SSM_HEADS,), f32, 1.0, 16.0))
    return {
        "x_prompt": nrm(3, (BATCH, SEQ, D_MODEL), 1.0),
        "x_sample": nrm(4, (DEC_BATCH, DEC_SEQ, D_MODEL), 1.0),
        "cache_k": nrm(5, (n_pool, PAGE_SIZE, N_KV_HEADS, 2 * HEAD_DIM), 1.0),
        "cache_v": nrm(6, (n_pool, PAGE_SIZE, N_KV_HEADS, 2 * HEAD_DIM), 1.0),
        "page_table": page_table,
        "state_conv": nrm(7, (DEC_BATCH, CONV_WIDTH - 1, CONV_DIM), 1.0),
        "state_ssm": nrm(8, (DEC_BATCH, SSM_HEADS, SSM_HEAD_DIM, D_STATE), 0.5),
        "rel_bias": nrm(9, (N_BUCKETS, N_HEADS), 0.5),
        "w_in": nrm(10, (D_MODEL, sum(IN_SIZES)), D_MODEL ** -0.5),
        "lambda_q1": nrm(11, (HEAD_DIM,), 0.1),
        "lambda_k1": nrm(12, (HEAD_DIM,), 0.1),
        "lambda_q2": nrm(13, (HEAD_DIM,), 0.1),
        "lambda_k2": nrm(14, (HEAD_DIM,), 0.1),
        "attn_subln": 1.0 + nrm(15, (2 * HEAD_DIM,), 0.02),
        "conv_w": nrm(16, (CONV_WIDTH, CONV_DIM), CONV_WIDTH ** -0.5),
        "conv_b": nrm(17, (CONV_DIM,), 0.02),
        "dt_bias": dt_bias,
        "a_log": a_log,
        "d_skip": 1.0 + nrm(18, (SSM_HEADS,), 0.1),
        "ssm_norm_w": 1.0 + nrm(19, (D_INNER,), 0.02),
        "w_branch_attn": nrm(20, (ATTN_WIDTH, D_MODEL), ATTN_WIDTH ** -0.5),
        "w_branch_ssm": nrm(21, (D_INNER, D_MODEL), D_INNER ** -0.5),
        "w_out": nrm(22, (D_MODEL, D_MODEL), BETA * D_MODEL ** -0.5),
        "ln1_g": 1.0 + nrm(23, (D_MODEL,), 0.02),
        "ln1_b": nrm(24, (D_MODEL,), 0.02),
        "w_route_group": nrm(25, (D_MODEL, MOE_GROUPS), D_MODEL ** -0.5),
        "b_route_group": nrm(26, (MOE_GROUPS,), 0.01),
        "w_route_expert": nrm(27, (D_MODEL, N_EXPERTS), D_MODEL ** -0.5),
        "b_route_expert": nrm(28, (N_EXPERTS,), 0.01),
        "w_gate": nrm(29, (N_EXPERTS, D_MODEL, D_FF_EXPERT), D_MODEL ** -0.5),
        "w_up": nrm(30, (N_EXPERTS, D_MODEL, D_FF_EXPERT), D_MODEL ** -0.5),
        "w_down": nrm(31, (N_EXPERTS, D_FF_EXPERT, D_MODEL), BETA * D_FF_EXPERT ** -0.5),
        "ln2_g": 1.0 + nrm(32, (D_MODEL,), 0.02),
        "ln2_b": nrm(33, (D_MODEL,), 0.02),
    }


def reference(x_prompt, x_sample, cache_k, cache_v, page_table, state_conv, state_ssm, rel_bias, w_in,
              lambda_q1, lambda_k1, lambda_q2, lambda_k2, attn_subln, conv_w, conv_b, dt_bias, a_log,
              d_skip, ssm_norm_w, w_branch_attn, w_branch_ssm, w_out, ln1_g, ln1_b, w_route_group,
              b_route_group, w_route_expert, b_route_expert, w_gate, w_up, w_down, ln2_g, ln2_b):
    f32 = jnp.float32
    q_scale = HEAD_DIM ** -0.5
    sample_chunk = math.gcd(DEC_SEQ, SSD_CHUNK)
    h_p, h_s = x_prompt, x_sample
    for layer_idx in range(DEPTH):
        lam_init = 0.8 - 0.6 * math.exp(-0.3 * layer_idx)
        lam = (jnp.exp(jnp.sum(lambda_q1.astype(f32) * lambda_k1.astype(f32)))
               - jnp.exp(jnp.sum(lambda_q2.astype(f32) * lambda_k2.astype(f32))) + lam_init)

        q, k, v, z, xbc, dt, ga, gb = split_projection(h_p, w_in)
        q_p = q.reshape(BATCH, SEQ, N_HEADS, 2, HEAD_DIM) * q_scale
        k_p = k.reshape(BATCH, SEQ, N_KV_HEADS, 2, HEAD_DIM)
        v_p = v.reshape(BATCH, SEQ, N_KV_HEADS, 2 * HEAD_DIM)
        k_pos_p = jnp.arange(SEQ)

        def prompt_block(i):
            q_blk = lax.dynamic_slice_in_dim(q_p, i * Q_BLOCK, Q_BLOCK, axis=1)
            q_pos = i * Q_BLOCK + jnp.arange(Q_BLOCK)
            return diff_attention(q_blk, k_p, v_p, q_pos, k_pos_p, rel_bias, lam, lam_init, attn_subln)

        attn_p = lax.map(prompt_block, jnp.arange(SEQ // Q_BLOCK))
        attn_p = jnp.moveaxis(attn_p, 0, 1).reshape(BATCH, SEQ, ATTN_WIDTH)
        conv0 = jnp.zeros((BATCH, CONV_WIDTH - 1, CONV_DIM), xbc.dtype)
        ssm0 = jnp.zeros((BATCH, SSM_HEADS, SSM_HEAD_DIM, D_STATE), f32)
        ssm_out_p, conv_prompt, ssm_prompt = mamba_mixer(z, xbc, dt, conv0, ssm0, SSD_CHUNK, conv_w, conv_b,
                                                         dt_bias, a_log, d_skip, ssm_norm_w)
        k_prompt = k.reshape(BATCH, SEQ, N_KV_HEADS, 2 * HEAD_DIM)
        v_prompt = v_p
        h_p = layer_tail(h_p, attn_p, ssm_out_p, ga, gb, w_branch_attn, w_branch_ssm, w_out, ln1_g, ln1_b,
                         w_route_group, b_route_group, w_route_expert, b_route_expert, w_gate, w_up, w_down,
                         ln2_g, ln2_b)

        q, k, v, z, xbc, dt, ga, gb = split_projection(h_s, w_in)
        q_s = q.reshape(DEC_BATCH, DEC_SEQ, N_HEADS, 2, HEAD_DIM) * q_scale
        k_s = k.reshape(DEC_BATCH, DEC_SEQ, N_KV_HEADS, 2, HEAD_DIM)
        v_s = v.reshape(DEC_BATCH, DEC_SEQ, N_KV_HEADS, 2 * HEAD_DIM)
        q_pos_s = PAST_LEN + jnp.arange(DEC_SEQ)
        k_pos_s = jnp.arange(PAST_LEN + DEC_SEQ)

        def sample_seq(args):
            q_i, k_i, v_i, pages = args
            k_past = cache_k[pages].reshape(PAST_LEN, N_KV_HEADS, 2, HEAD_DIM).astype(k_i.dtype)
            v_past = cache_v[pages].reshape(PAST_LEN, N_KV_HEADS, 2 * HEAD_DIM).astype(v_i.dtype)
            k_all = jnp.concatenate([k_past, k_i], axis=0)[None]
            v_all = jnp.concatenate([v_past, v_i], axis=0)[None]
            return diff_attention(q_i[None], k_all, v_all, q_pos_s, k_pos_s, rel_bias, lam, lam_init,
                                  attn_subln)[0]

        attn_s = lax.map(sample_seq, (q_s, k_s, v_s, page_table))
        ssm_out_s, conv_sample, ssm_sample = mamba_mixer(z, xbc, dt, state_conv, state_ssm, sample_chunk,
                                                         conv_w, conv_b, dt_bias, a_log, d_skip, ssm_norm_w)
        k_sample = k.reshape(DEC_BATCH, DEC_SEQ, N_KV_HEADS, 2 * HEAD_DIM)
        v_sample = v_s
        h_s = layer_tail(h_s, attn_s, ssm_out_s, ga, gb, w_branch_attn, w_branch_ssm, w_out, ln1_g, ln1_b,
                         w_route_group, b_route_group, w_route_expert, b_route_expert, w_gate, w_up, w_down,
                         ln2_g, ln2_b)
    return (h_p, h_s, k_prompt, v_prompt, conv_prompt, ssm_prompt, k_sample, v_sample, conv_sample, ssm_sample)
```

```python
import functools
import math

import jax
import jax.numpy as jnp
import numpy as np
from jax import lax
from jax.experimental import pallas as pl
from jax.experimental.pallas import tpu as pltpu

F32 = jnp.float32
BF16 = jnp.bfloat16
I32 = jnp.int32

D_MODEL = 2048
DEPTH = 1
N_HEADS = 8
N_KV_HEADS = 4
Q_PER_KV = N_HEADS // N_KV_HEADS
HEAD_DIM = D_MODEL // (2 * N_HEADS)
V_DIM = 2 * HEAD_DIM
ATTN_WIDTH = N_HEADS * V_DIM
KV_WIDTH = N_KV_HEADS * V_DIM
N_BUCKETS = 32
MAX_EXACT = N_BUCKETS // 2
MAX_DISTANCE = 128
D_INNER = 2 * D_MODEL
SSM_HEAD_DIM = 64
SSM_HEADS = D_INNER // SSM_HEAD_DIM
SSM_GROUPS = 8
SSM_HPG = SSM_HEADS // SSM_GROUPS
D_STATE = 128
CONV_WIDTH = 4
GN = SSM_GROUPS * D_STATE
CONV_DIM = D_INNER + 2 * GN
SSD_CHUNK = 128
GROUP_WIDTH = D_INNER // SSM_GROUPS
MOE_GROUPS = 4
EXPERTS_PER_GROUP = 8
N_EXPERTS = MOE_GROUPS * EXPERTS_PER_GROUP
TOP_K = 2
D_FF = D_MODEL // 2
MOE_BLOCK = 128
ALPHA = (2 * DEPTH) ** 0.25
EPS = 1e-5
LAM_INIT = 0.8 - 0.6 * math.exp(-0.3 * 0)
Q_SCALE = HEAD_DIM ** -0.5
OFF_Q, OFF_K, OFF_V, OFF_Z, OFF_XBC = 0, 2048, 3072, 4096, 8192
OFF_DT = OFF_XBC + CONV_DIM
OFF_GATES = OFF_DT + SSM_HEADS

LANES = 128
V7X_VMEM_LIMIT = 56 * 1024 * 1024


def _cparams(n_axes, vmem=V7X_VMEM_LIMIT):
    return pltpu.CompilerParams(
        dimension_semantics=("arbitrary",) * n_axes, vmem_limit_bytes=vmem)


def _sigmoid(x):
    return 1.0 / (1.0 + jnp.exp(-x))


def _silu(x):
    return x * _sigmoid(x)


def _softplus(x):
    return jnp.maximum(x, 0.0) + jnp.log(1.0 + jnp.exp(-jnp.abs(x)))


def _layer_norm(x, g, b):
    xc = x - jnp.mean(x, axis=-1, keepdims=True)
    var = jnp.mean(xc * xc, axis=-1, keepdims=True)
    return xc * lax.rsqrt(var + EPS) * g + b


def _lane_tile(x, n):
    return jnp.concatenate([x] * n, axis=1)


def _split3(v):
    hi = v.astype(BF16)
    r1 = v - hi.astype(F32)
    mid = r1.astype(BF16)
    lo = (r1 - mid.astype(F32)).astype(BF16)
    return jnp.concatenate([hi, mid, lo], axis=1)


def _proj_kernel(x_ref, w_ref, *o_refs, scale):
    acc = jnp.dot(x_ref[...], w_ref[...], preferred_element_type=F32)
    if scale != 1.0:
        acc = acc * scale
    for o_ref in o_refs:
        o_ref[...] = acc.astype(o_ref.dtype)


def _proj(x, w, col0, ncols, out_dtypes, *, scale=1.0, tm, tn, name):
    t, k = x.shape
    off = col0 // tn
    assert col0 % tn == 0 and ncols % tn == 0 and t % tm == 0
    return pl.pallas_call(
        functools.partial(_proj_kernel, scale=scale),
        grid=(t // tm, ncols // tn),
        in_specs=[pl.BlockSpec((tm, k), lambda i, j: (i, 0)),
                  pl.BlockSpec((k, tn), lambda i, j: (0, j + off))],
        out_specs=[pl.BlockSpec((tm, tn), lambda i, j: (i, j)) for _ in out_dtypes],
        out_shape=[jax.ShapeDtypeStruct((t, ncols), dt) for dt in out_dtypes],
        compiler_params=_cparams(2),
        name=name,
    )(x, w)


def _in_projection(x_bf, w_main, w_dt, w_gates, tm):
    tn = 1024
    (q,) = _proj(x_bf, w_main, OFF_Q, ATTN_WIDTH, [BF16], scale=Q_SCALE, tm=tm, tn=tn, name="proj_q")
    k, k_bf = _proj(x_bf, w_main, OFF_K, KV_WIDTH, [F32, BF16], tm=tm, tn=tn, name="proj_k")
    v, v_bf = _proj(x_bf, w_main, OFF_V, KV_WIDTH, [F32, BF16], tm=tm, tn=tn, name="proj_v")
    (z,) = _proj(x_bf, w_main, OFF_Z, D_INNER, [F32], tm=tm, tn=tn, name="proj_z")
    (xbc,) = _proj(x_bf, w_main, OFF_XBC, CONV_DIM, [F32], tm=tm, tn=tn, name="proj_xbc")
    (dt,) = _proj(x_bf, w_dt, 0, LANES, [F32], tm=tm, tn=LANES, name="proj_dt")
    (gates,) = _proj(x_bf, w_gates, 0, 2 * D_MODEL, [F32], tm=tm, tn=tn, name="proj_gates")
    return q, k, k_bf, v, v_bf, z, xbc, dt, gates


def _t5_bucket(dist):
    n = jnp.maximum(dist, 0)
    nf = jnp.maximum(n, 1).astype(F32)
    large = MAX_EXACT + (jnp.log(nf / MAX_EXACT) / math.log(MAX_DISTANCE / MAX_EXACT)
                         * (N_BUCKETS - MAX_EXACT)).astype(I32)
    large = jnp.minimum(large, N_BUCKETS - 1)
    return jnp.where(n < MAX_EXACT, n, large)


def _bias_minus_far(dist, relb_ref, head):
    bucket = _t5_bucket(dist)
    far = relb_ref[N_BUCKETS - 1, head]
    out = jnp.zeros(dist.shape, F32)
    for b in range(N_BUCKETS - 1):
        out = jnp.where(bucket == b, relb_ref[b, head] - far, out)
    return out


def _lambda_value(lam_ref):
    s1 = jnp.sum(lam_ref[0:1, :] * lam_ref[1:2, :], axis=1, keepdims=True)
    s2 = jnp.sum(lam_ref[2:3, :] * lam_ref[3:4, :], axis=1, keepdims=True)
    return jnp.exp(s1) - jnp.exp(s2) + LAM_INIT


def _subln(o, subln_ref):
    ms = jnp.mean(o * o, axis=-1, keepdims=True)
    return o * lax.rsqrt(ms + EPS) * subln_ref[...] * (1.0 - LAM_INIT)


def _prompt_attn_kernel(relb_ref, lam_ref, subln_ref, q_ref, k_ref, v_ref, o_ref,
                        qs_ref, m_ref, l_ref, acc_ref, bias_ref, *, tq):
    g = pl.program_id(0)
    b = pl.program_id(1)
    qi = pl.program_id(2)
    rows = 2 * tq

    @pl.when(jnp.logical_and(b == 0, qi == 0))
    def _():
        r = lax.broadcasted_iota(I32, (tq, tq), 0)
        c = lax.broadcasted_iota(I32, (tq, tq), 1)
        for hh in range(Q_PER_KV):
            head = g * Q_PER_KV + hh
            adj = _bias_minus_far(tq + r - c, relb_ref, head)
            diag = _bias_minus_far(r - c, relb_ref, head)
            diag = jnp.where(r >= c, diag, -jnp.inf)
            bias_ref[0, hh * tq:(hh + 1) * tq, :] = adj
            bias_ref[1, hh * tq:(hh + 1) * tq, :] = diag

    for hh in range(Q_PER_KV):
        for mm in range(2):
            c0 = hh * V_DIM + mm * HEAD_DIM
            qs_ref[mm, hh * tq:(hh + 1) * tq, :] = q_ref[0, :, c0:c0 + HEAD_DIM]

    m_ref[...] = jnp.full(m_ref.shape, -jnp.inf, F32)
    l_ref[...] = jnp.zeros(l_ref.shape, F32)
    acc_ref[...] = jnp.zeros(acc_ref.shape, F32)

    def tile_update(mm, ki, bias):
        start = pl.multiple_of(ki * tq, tq)
        kt = k_ref[0, pl.ds(start, tq), mm * HEAD_DIM:(mm + 1) * HEAD_DIM]
        vt = v_ref[0, pl.ds(start, tq), :]
        s = lax.dot_general(qs_ref[mm], kt, (((1,), (1,)), ((), ())),
                            preferred_element_type=F32)
        if bias is not None:
            s = s + bias
        m_prev = m_ref[mm]
        m_new = jnp.maximum(m_prev, jnp.max(s, axis=1, keepdims=True))
        p = jnp.exp(s - _lane_tile(m_new, tq // LANES))
        alpha = jnp.exp(m_prev - m_new)
        l_ref[mm] = alpha * l_ref[mm] + jnp.sum(p, axis=1, keepdims=True)
        acc_ref[mm] = (acc_ref[mm] * _lane_tile(alpha, V_DIM // LANES)
                       + jnp.dot(p.astype(BF16), vt, preferred_element_type=F32))
        m_ref[mm] = m_new

    for mm in range(2):
        def far_body(ki, carry, mm=mm):
            tile_update(mm, ki, None)
            return carry
        lax.fori_loop(0, qi - 1, far_body, 0)

        @pl.when(qi >= 1)
        def _(mm=mm):
            tile_update(mm, qi - 1, bias_ref[0])

        tile_update(mm, qi, bias_ref[1])

    lam = _lambda_value(lam_ref)
    o1 = acc_ref[0] / _lane_tile(l_ref[0], V_DIM // LANES)
    o2 = acc_ref[1] / _lane_tile(l_ref[1], V_DIM // LANES)
    o = _subln(o1 - lam * o2, subln_ref)
    for hh in range(Q_PER_KV):
        o_ref[0, :, hh * V_DIM:(hh + 1) * V_DIM] = o[hh * tq:(hh + 1) * tq].astype(o_ref.dtype)


def _prompt_attention(q, k_bf, v_bf, rel_bias, lam_rows, subln, *, tq):
    bsz, length, _ = q.shape
    assert length % tq == 0 and tq % LANES == 0
    gw = Q_PER_KV * V_DIM
    rows = 2 * tq
    return pl.pallas_call(
        functools.partial(_prompt_attn_kernel, tq=tq),
        grid=(N_KV_HEADS, bsz, length // tq),
        in_specs=[
            pl.BlockSpec(memory_space=pltpu.SMEM),
            pl.BlockSpec((4, HEAD_DIM), lambda g, b, i: (0, 0)),
            pl.BlockSpec((1, V_DIM), lambda g, b, i: (0, 0)),
            pl.BlockSpec((1, tq, gw), lambda g, b, i: (b, i, g)),
            pl.BlockSpec((1, length, V_DIM), lambda g, b, i: (b, 0, g)),
            pl.BlockSpec((1, length, V_DIM), lambda g, b, i: (b, 0, g)),
        ],
        out_specs=pl.BlockSpec((1, tq, gw), lambda g, b, i: (b, i, g)),
        out_shape=jax.ShapeDtypeStruct((bsz, length, ATTN_WIDTH), BF16),
        scratch_shapes=[
            pltpu.VMEM((2, rows, HEAD_DIM), BF16),
            pltpu.VMEM((2, rows, LANES), F32),
            pltpu.VMEM((2, rows, LANES), F32),
            pltpu.VMEM((2, rows, V_DIM), F32),
            pltpu.VMEM((2, rows, tq), F32),
        ],
        compiler_params=_cparams(3),
        name="prompt_attention",
    )(rel_bias, lam_rows, subln, q, k_bf, v_bf)


def _decode_attn_kernel(pt_ref, relb_ref, lam_ref, subln_ref, q_ref, kn_ref, vn_ref, *rest,
                        pages_per_step, page, past_len):
    del pt_ref
    kp = rest[:pages_per_step]
    vp = rest[pages_per_step:2 * pages_per_step]
    o_ref, qd_ref, m_ref, l_ref, acc_ref = rest[2 * pages_per_step:]
    j = pl.program_id(1)
    n_steps = pl.num_programs(1)
    span = pages_per_step * page
    qrows = 8

    @pl.when(j == 0)
    def _():
        qd_ref[...] = jnp.zeros(qd_ref.shape, BF16)
        for g in range(N_KV_HEADS):
            for hh in range(Q_PER_KV):
                for mm in range(2):
                    src = (g * Q_PER_KV + hh) * 2 + mm
                    qd_ref[g, mm * Q_PER_KV + hh:mm * Q_PER_KV + hh + 1,
                           mm * HEAD_DIM:(mm + 1) * HEAD_DIM] = q_ref[0, src:src + 1, :]
        m_ref[...] = jnp.full(m_ref.shape, -jnp.inf, F32)
        l_ref[...] = jnp.zeros(l_ref.shape, F32)
        acc_ref[...] = jnp.zeros(acc_ref.shape, F32)

    def head_of_row(g):
        r = lax.broadcasted_iota(I32, (qrows, LANES), 0)
        return g * Q_PER_KV + r % Q_PER_KV

    def row_bias(g, dist):
        out = jnp.zeros(dist.shape, F32)
        r = lax.broadcasted_iota(I32, dist.shape, 0)
        for hh in range(Q_PER_KV):
            bh = _bias_minus_far(dist, relb_ref, g * Q_PER_KV + hh)
            out = jnp.where(r % Q_PER_KV == hh, bh, out)
        return out

    def online(g, s, pv_fn):
        m_prev = m_ref[g]
        m_new = jnp.maximum(m_prev, jnp.max(s, axis=1, keepdims=True))
        p = jnp.exp(s - m_new[:, :1])
        alpha = jnp.exp(m_prev - m_new)
        l_ref[g] = alpha * l_ref[g] + jnp.sum(p, axis=1, keepdims=True)
        acc_ref[g] = acc_ref[g] * _lane_tile(alpha, V_DIM // LANES) + pv_fn(p)
        m_ref[g] = m_new

    for g in range(N_KV_HEADS):
        qd = qd_ref[g]
        lo, hi = g * V_DIM, (g + 1) * V_DIM
        s_parts = [lax.dot_general(qd, kp[t][0, :, lo:hi].astype(BF16),
                                   (((1,), (1,)), ((), ())), preferred_element_type=F32)
                   for t in range(pages_per_step)]
        s = jnp.concatenate(s_parts, axis=1)

        def add_near_bias(s, g=g):
            c = lax.broadcasted_iota(I32, (qrows, span), 1)
            dist = past_len - (j * span + c)
            return s + row_bias(g, dist)

        s = lax.cond(j == n_steps - 1, add_near_bias, lambda s: s, s)

        def pv(p, lo=lo, hi=hi):
            out = jnp.zeros((qrows, V_DIM), F32)
            for t in range(pages_per_step):
                out = out + jnp.dot(p[:, t * page:(t + 1) * page].astype(BF16),
                                    vp[t][0, :, lo:hi].astype(BF16), preferred_element_type=F32)
            return out

        online(g, s, pv)

    @pl.when(j == n_steps - 1)
    def _():
        lam = _lambda_value(lam_ref)
        for g in range(N_KV_HEADS):
            lo, hi = g * V_DIM, (g + 1) * V_DIM
            kn = kn_ref[0, :, lo:hi].astype(BF16).astype(F32)
            vn = vn_ref[0, :, lo:hi].astype(BF16).astype(F32)
            s_new = jnp.sum(qd_ref[g].astype(F32) * kn, axis=1, keepdims=True)
            s_new = s_new + row_bias(g, jnp.zeros((qrows, 1), I32))
            online(g, s_new, lambda p, vn=vn: p.astype(BF16).astype(F32) * vn)
            o_all = acc_ref[g] / _lane_tile(l_ref[g], V_DIM // LANES)
            o = o_all[0:Q_PER_KV] - lam * o_all[Q_PER_KV:2 * Q_PER_KV]
            o = _subln(o, subln_ref)
            for hh in range(Q_PER_KV):
                h = g * Q_PER_KV + hh
                o_ref[0, :, h * V_DIM:(h + 1) * V_DIM] = o[hh:hh + 1].astype(o_ref.dtype)


def _decode_attention(q, k_new, v_new, cache_k, cache_v, page_table, rel_bias, lam_rows, subln,
                      *, pages_per_step):
    n_seq = q.shape[0]
    _, page, _ = cache_k.shape
    n_pages = page_table.shape[1]
    assert n_pages % pages_per_step == 0
    past_len = n_pages * page
    q3 = q.reshape(n_seq, N_HEADS * 2, HEAD_DIM)
    kn3 = k_new.reshape(n_seq, 1, KV_WIDTH)
    vn3 = v_new.reshape(n_seq, 1, KV_WIDTH)

    def page_spec(t):
        return pl.BlockSpec((1, page, KV_WIDTH),
                            lambda i, j, pt, t=t: (pt[i, j * pages_per_step + t], 0, 0))

    grid_spec = pltpu.PrefetchScalarGridSpec(
        num_scalar_prefetch=1,
        grid=(n_seq, n_pages // pages_per_step),
        in_specs=[
            pl.BlockSpec(memory_space=pltpu.SMEM),
            pl.BlockSpec((4, HEAD_DIM), lambda i, j, pt: (0, 0)),
            pl.BlockSpec((1, V_DIM), lambda i, j, pt: (0, 0)),
            pl.BlockSpec((1, N_HEADS * 2, HEAD_DIM), lambda i, j, pt: (i, 0, 0)),
            pl.BlockSpec((1, 1, KV_WIDTH), lambda i, j, pt: (i, 0, 0)),
            pl.BlockSpec((1, 1, KV_WIDTH), lambda i, j, pt: (i, 0, 0)),
        ] + [page_spec(t) for t in range(pages_per_step)] * 2,
        out_specs=pl.BlockSpec((1, 1, ATTN_WIDTH), lambda i, j, pt: (i, 0, 0)),
        scratch_shapes=[
            pltpu.VMEM((N_KV_HEADS, 8, V_DIM), BF16),
            pltpu.VMEM((N_KV_HEADS, 8, LANES), F32),
            pltpu.VMEM((N_KV_HEADS, 8, LANES), F32),
            pltpu.VMEM((N_KV_HEADS, 8, V_DIM), F32),
        ],
    )
    out = pl.pallas_call(
        functools.partial(_decode_attn_kernel, pages_per_step=pages_per_step, page=page,
                          past_len=past_len),
        grid_spec=grid_spec,
        out_shape=jax.ShapeDtypeStruct((n_seq, 1, ATTN_WIDTH), BF16),
        compiler_params=_cparams(2),
        name="decode_attention",
    )(page_table, rel_bias, lam_rows, subln, q3, kn3, vn3,
      *([cache_k] * pages_per_step), *([cache_v] * pages_per_step))
    return out.reshape(n_seq, ATTN_WIDTH)


def _gate_and_norm(y, z, normw):
    y = y * _silu(z)
    parts = []
    for g in range(SSM_GROUPS):
        yg = y[:, g * GROUP_WIDTH:(g + 1) * GROUP_WIDTH]
        ms = jnp.mean(yg * yg, axis=-1, keepdims=True)
        parts.append(yg * lax.rsqrt(ms + EPS))
    return jnp.concatenate(parts, axis=1) * normw


def _ssd_prompt_kernel(xbc_ref, z_ref, dt_ref, convw_ref, convb_ref, dtb_ref, alog_ref,
                       dskip_ref, normw_ref, o_ref, state_ref,
                       ext_ref, act_ref, st_ref, y_ref, e3_ref, *, chunk):
    c = pl.program_id(1)
    n_chunks = pl.num_programs(1)
    halo = CONV_WIDTH - 1
    base = 8
    pair_w = 2 * SSM_HEAD_DIM

    @pl.when(c == 0)
    def _():
        ext_ref[0:base, :] = jnp.zeros((base, CONV_DIM), F32)
        st_ref[...] = jnp.zeros(st_ref.shape, F32)
        r = lax.broadcasted_iota(I32, e3_ref.shape, 0) % LANES
        col = lax.broadcasted_iota(I32, e3_ref.shape, 1)
        e3_ref[...] = jnp.where(col // SSM_HEAD_DIM == r, 1.0, 0.0).astype(BF16)

    ext_ref[base:base + chunk, :] = xbc_ref[0]
    col_tile = 512
    for j in range(CONV_DIM // col_tile):
        cs = slice(j * col_tile, (j + 1) * col_tile)
        conv = convb_ref[:, cs] + jnp.zeros((chunk, col_tile), F32)
        for w in range(CONV_WIDTH):
            lo = base - halo + w
            conv = conv + ext_ref[lo:lo + chunk, cs] * convw_ref[w:w + 1, cs]
        act_ref[:, cs] = _silu(conv)
    ext_ref[base - halo:base, :] = ext_ref[base + chunk - halo:base + chunk, :]

    dt_t = _softplus(jnp.transpose(dt_ref[0]) + dtb_ref[...])
    a_t = dt_t * (-jnp.exp(alog_ref[...]))
    tt = lax.broadcasted_iota(I32, (chunk, chunk), 0)
    ss = lax.broadcasted_iota(I32, (chunk, chunk), 1)
    upper = jnp.where(tt <= ss, 1.0, 0.0).astype(BF16)
    upper3 = jnp.concatenate([upper, upper, upper], axis=0)
    acs_t = jnp.dot(_split3(a_t), upper3, preferred_element_type=F32)
    acs = jnp.transpose(acs_t)
    w_t = dt_t * jnp.exp(acs_t[:, chunk - 1:chunk] - acs_t)
    eexp = jnp.dot(_split3(jnp.exp(acs)), e3_ref[...],
                   preferred_element_type=F32)
    causal = tt >= ss

    for g in range(SSM_GROUPS):
        bg = act_ref[:, D_INNER + g * D_STATE:D_INNER + (g + 1) * D_STATE]
        cg = act_ref[:, D_INNER + GN + g * D_STATE:D_INNER + GN + (g + 1) * D_STATE]
        bg_bf = bg.astype(BF16)
        cg_bf = cg.astype(BF16)
        cb = lax.dot_general(cg_bf, bg_bf, (((1,), (1,)), ((), ())),
                             preferred_element_type=F32)
        bg_t = jnp.transpose(bg)
        gsl = slice(g * GROUP_WIDTH, (g + 1) * GROUP_WIDTH)
        y_off = jnp.dot(cg_bf, st_ref[:, gsl].astype(BF16), preferred_element_type=F32)
        for pr in range(SSM_HPG // 2):
            h0 = g * SSM_HPG + 2 * pr
            psl = slice(h0 * SSM_HEAD_DIM, h0 * SSM_HEAD_DIM + pair_w)
            lhs_d, lhs_s = [], []
            for h in (h0, h0 + 1):
                seg = acs[:, h:h + 1] - acs_t[h:h + 1, :]
                dec = jnp.exp(jnp.where(causal, seg, -jnp.inf))
                lhs_d.append((cb * dec * dt_t[h:h + 1, :]).astype(BF16))
                lhs_s.append((bg_t * w_t[h:h + 1, :]).astype(BF16))
            lhs = jnp.concatenate([jnp.concatenate(lhs_d, axis=1),
                                   jnp.concatenate(lhs_s, axis=1)], axis=0)
            xs = act_ref[:, psl]
            lane = lax.broadcasted_iota(I32, xs.shape, 1)
            rhs = jnp.concatenate([jnp.where(lane < SSM_HEAD_DIM, xs, 0.0),
                                   jnp.where(lane >= SSM_HEAD_DIM, xs, 0.0)], axis=0).astype(BF16)
            res = jnp.dot(lhs, rhs, preferred_element_type=F32)
            y_ref[:, psl] = (res[:chunk] + y_off[:, 2 * pr * SSM_HEAD_DIM:2 * pr * SSM_HEAD_DIM + pair_w]
                             * eexp[:, psl] + dskip_ref[:, psl] * xs)
            st_ref[:, psl] = st_ref[:, psl] * eexp[chunk - 1:chunk, psl] + res[chunk:]

    o_ref[0] = _gate_and_norm(y_ref[...], z_ref[0], normw_ref[...]).astype(o_ref.dtype)

    @pl.when(c == n_chunks - 1)
    def _():
        for j in range(D_INNER // LANES):
            state_ref[0, j * LANES:(j + 1) * LANES, :] = jnp.transpose(
                st_ref[:, j * LANES:(j + 1) * LANES])


def _ssd_prompt(xbc, z, dt, conv_w, conv_b, dtb_col, alog_col, dskip_row, norm_w):
    bsz, length, _ = xbc.shape
    chunk = SSD_CHUNK
    assert length % chunk == 0

    def const(shape):
        return pl.BlockSpec(shape, lambda b, c: (0,) * len(shape))

    return pl.pallas_call(
        functools.partial(_ssd_prompt_kernel, chunk=chunk),
        grid=(bsz, length // chunk),
        in_specs=[
            pl.BlockSpec((1, chunk, CONV_DIM), lambda b, c: (b, c, 0)),
            pl.BlockSpec((1, chunk, D_INNER), lambda b, c: (b, c, 0)),
            pl.BlockSpec((1, chunk, LANES), lambda b, c: (b, c, 0)),
            const((CONV_WIDTH, CONV_DIM)), const((1, CONV_DIM)),
            const((LANES, 1)), const((LANES, 1)),
            const((1, D_INNER)), const((1, D_INNER)),
        ],
        out_specs=[
            pl.BlockSpec((1, chunk, D_INNER), lambda b, c: (b, c, 0)),
            pl.BlockSpec((1, D_INNER, D_STATE), lambda b, c: (b, 0, 0)),
        ],
        out_shape=[
            jax.ShapeDtypeStruct((bsz, length, D_INNER), BF16),
            jax.ShapeDtypeStruct((bsz, D_INNER, D_STATE), F32),
        ],
        scratch_shapes=[
            pltpu.VMEM((8 + chunk, CONV_DIM), F32),
            pltpu.VMEM((chunk, CONV_DIM), F32),
            pltpu.VMEM((D_STATE, D_INNER), F32),
            pltpu.VMEM((chunk, D_INNER), F32),
            pltpu.VMEM((3 * LANES, D_INNER), BF16),
        ],
        compiler_params=_cparams(2),
        name="ssd_prompt",
    )(xbc, z, dt, conv_w, conv_b, dtb_col, alog_col, dskip_row, norm_w)


def _row_to_col(row):
    r = lax.broadcasted_iota(I32, (LANES, LANES), 0)
    c = lax.broadcasted_iota(I32, (LANES, LANES), 1)
    return jnp.sum(jnp.where(r == c, jnp.broadcast_to(row, (LANES, LANES)), 0.0),
                   axis=1, keepdims=True)


def _ssd_step_kernel(xbc_ref, sconv_ref, z_ref, dt_ref, state_ref, convw_ref, convb_ref,
                     dtb_ref, alog_ref, dskip_ref, normw_ref,
                     o_ref, conv_out_ref, state_out_ref, e3_ref):
    halo = CONV_WIDTH - 1

    @pl.when(pl.program_id(0) == 0)
    def _():
        r = lax.broadcasted_iota(I32, e3_ref.shape, 0) % LANES
        col = lax.broadcasted_iota(I32, e3_ref.shape, 1)
        e3_ref[...] = jnp.where(col // SSM_HEAD_DIM == r, 1.0, 0.0).astype(BF16)

    x_new = xbc_ref[0]
    conv = convb_ref[...] + x_new * convw_ref[halo:halo + 1, :]
    for w in range(halo):
        conv = conv + sconv_ref[0, w:w + 1, :] * convw_ref[w:w + 1, :]
    act = _silu(conv)
    conv_out_ref[0, 0:halo - 1, :] = sconv_ref[0, 1:halo, :]
    conv_out_ref[0, halo - 1:halo, :] = x_new

    dt = _softplus(dt_ref[0] + dtb_ref[...])
    decay = jnp.exp(dt * (-jnp.exp(alog_ref[...])))
    both = jnp.concatenate([decay, dt, jnp.zeros((6, LANES), F32)], axis=0)
    expanded = jnp.dot(_split3(both), e3_ref[...], preferred_element_type=F32)
    decay_x = expanded[0:1]
    xs = act[:, :D_INNER]
    xdt = xs * expanded[1:2]

    y_parts = []
    for g in range(SSM_GROUPS):
        b_row = act[:, D_INNER + g * D_STATE:D_INNER + (g + 1) * D_STATE]
        c_row = act[:, D_INNER + GN + g * D_STATE:D_INNER + GN + (g + 1) * D_STATE]
        for k in range(GROUP_WIDTH // LANES):
            r0 = g * GROUP_WIDTH + k * LANES
            new = (state_ref[0, r0:r0 + LANES, :] * _row_to_col(decay_x[:, r0:r0 + LANES])
                   + _row_to_col(xdt[:, r0:r0 + LANES]) * b_row)
            state_out_ref[0, r0:r0 + LANES, :] = new
        h_new = state_out_ref[0, g * GROUP_WIDTH:(g + 1) * GROUP_WIDTH, :].astype(BF16)
        c8 = jnp.broadcast_to(c_row, (8, D_STATE)).astype(BF16)
        y_parts.append(lax.dot_general(c8, h_new, (((1,), (1,)), ((), ())),
                                       preferred_element_type=F32)[0:1])
    y = jnp.concatenate(y_parts, axis=1) + dskip_ref[...] * xs
    o_ref[0] = _gate_and_norm(y, z_ref[0], normw_ref[...]).astype(o_ref.dtype)


def _ssd_step(xbc, state_conv, z, dt, state_ssm, conv_w, conv_b, dtb_row, alog_row, dskip_row, norm_w):
    n_seq = xbc.shape[0]
    halo = CONV_WIDTH - 1

    def const(shape):
        return pl.BlockSpec(shape, lambda i: (0,) * len(shape))

    y, conv_out, state_out = pl.pallas_call(
        _ssd_step_kernel,
        grid=(n_seq,),
        in_specs=[
            pl.BlockSpec((1, 1, CONV_DIM), lambda i: (i, 0, 0)),
            pl.BlockSpec((1, halo, CONV_DIM), lambda i: (i, 0, 0)),
            pl.BlockSpec((1, 1, D_INNER), lambda i: (i, 0, 0)),
            pl.BlockSpec((1, 1, LANES), lambda i: (i, 0, 0)),
            pl.BlockSpec((1, D_INNER, D_STATE), lambda i: (i, 0, 0)),
            const((CONV_WIDTH, CONV_DIM)), const((1, CONV_DIM)),
            const((1, LANES)), const((1, LANES)),
            const((1, D_INNER)), const((1, D_INNER)),
        ],
        out_specs=[
            pl.BlockSpec((1, 1, D_INNER), lambda i: (i, 0, 0)),
            pl.BlockSpec((1, halo, CONV_DIM), lambda i: (i, 0, 0)),
            pl.BlockSpec((1, D_INNER, D_STATE), lambda i: (i, 0, 0)),
        ],
        out_shape=[
            jax.ShapeDtypeStruct((n_seq, 1, D_INNER), BF16),
            jax.ShapeDtypeStruct((n_seq, halo, CONV_DIM), F32),
            jax.ShapeDtypeStruct((n_seq, D_INNER, D_STATE), F32),
        ],
        scratch_shapes=[pltpu.VMEM((3 * LANES, D_INNER), BF16)],
        compiler_params=_cparams(1),
        name="ssd_step",
    )(xbc.reshape(n_seq, 1, CONV_DIM), state_conv, z.reshape(n_seq, 1, D_INNER),
      dt.reshape(n_seq, 1, LANES), state_ssm, conv_w, conv_b, dtb_row, alog_row, dskip_row, norm_w)
    return y.reshape(n_seq, D_INNER), conv_out, state_out


def _merge_kernel(attn_ref, ssm_ref, wa_ref, ws_ref, ga_ref, gb_ref, o_ref):
    a = jnp.dot(attn_ref[...], wa_ref[...], preferred_element_type=F32)
    s = jnp.dot(ssm_ref[...], ws_ref[...], preferred_element_type=F32)
    o_ref[...] = (_sigmoid(ga_ref[...]) * a + _sigmoid(gb_ref[...]) * s).astype(o_ref.dtype)


def _merge(attn_o, ssm_o, gates, w_attn, w_ssm, *, tm, tn):
    t = attn_o.shape[0]
    nb = D_MODEL // tn
    return pl.pallas_call(
        _merge_kernel,
        grid=(t // tm, nb),
        in_specs=[
            pl.BlockSpec((tm, ATTN_WIDTH), lambda i, j: (i, 0)),
            pl.BlockSpec((tm, D_INNER), lambda i, j: (i, 0)),
            pl.BlockSpec((ATTN_WIDTH, tn), lambda i, j: (0, j)),
            pl.BlockSpec((D_INNER, tn), lambda i, j: (0, j)),
            pl.BlockSpec((tm, tn), lambda i, j: (i, j)),
            pl.BlockSpec((tm, tn), lambda i, j: (i, j + nb)),
        ],
        out_specs=pl.BlockSpec((tm, tn), lambda i, j: (i, j)),
        out_shape=jax.ShapeDtypeStruct((t, D_MODEL), BF16),
        compiler_params=_cparams(2),
        name="branch_merge",
    )(attn_o, ssm_o, w_attn, w_ssm, gates, gates)


def _ln1_router_kernel(m_ref, wout_ref, h_ref, g_ref, b_ref, wr_ref, br_ref,
                       h1_ref, ids_ref, wts_ref):
    x = ALPHA * h_ref[...] + jnp.dot(m_ref[...], wout_ref[...], preferred_element_type=F32)
    h1 = _layer_norm(x, g_ref[...], b_ref[...])
    h1_ref[...] = h1

    logits = jnp.dot(h1, wr_ref[...], preferred_element_type=F32,
                     precision=lax.Precision.HIGHEST) + br_ref[...]
    lane = lax.broadcasted_iota(I32, logits.shape, 1)
    big = jnp.int32(LANES)
    neg = -jnp.inf
    gl = jnp.where(lane < MOE_GROUPS, logits, neg)
    gmax = jnp.max(gl, axis=1, keepdims=True)
    g_idx = jnp.min(jnp.where(gl == gmax, lane, big), axis=1, keepdims=True)
    g_w = 1.0 / jnp.sum(jnp.exp(gl - gmax), axis=1, keepdims=True)
    e_lo = MOE_GROUPS + g_idx * EXPERTS_PER_GROUP
    el = jnp.where(jnp.logical_and(lane >= e_lo, lane < e_lo + EXPERTS_PER_GROUP), logits, neg)
    e1 = jnp.max(el, axis=1, keepdims=True)
    i1 = jnp.min(jnp.where(el == e1, lane, big), axis=1, keepdims=True)
    el2 = jnp.where(lane == i1, neg, el)
    e2 = jnp.max(el2, axis=1, keepdims=True)
    i2 = jnp.min(jnp.where(el2 == e2, lane, big), axis=1, keepdims=True)
    t2 = jnp.exp(e2 - e1)
    den = 1.0 + t2
    w1 = g_w * (1.0 / den)
    w2 = g_w * (t2 / den)
    ids_ref[...] = jnp.where(lane == 0, i1 - MOE_GROUPS, jnp.where(lane == 1, i2 - MOE_GROUPS, 0))
    wts_ref[...] = jnp.where(lane == 0, w1, jnp.where(lane == 1, w2, 0.0))


def _ln1_router(merged, w_out, h, ln_g, ln_b, w_route, b_route, *, tm):
    t = merged.shape[0]

    def const(shape):
        return pl.BlockSpec(shape, lambda i: (0,) * len(shape))

    def rows(width):
        return pl.BlockSpec((tm, width), lambda i: (i, 0))

    return pl.pallas_call(
        _ln1_router_kernel,
        grid=(t // tm,),
        in_specs=[rows(D_MODEL), const((D_MODEL, D_MODEL)), rows(D_MODEL),
                  const((1, D_MODEL)), const((1, D_MODEL)),
                  const((D_MODEL, LANES)), const((1, LANES))],
        out_specs=[rows(D_MODEL), rows(LANES), rows(LANES)],
        out_shape=[jax.ShapeDtypeStruct((t, D_MODEL), F32),
                   jax.ShapeDtypeStruct((t, LANES), I32),
                   jax.ShapeDtypeStruct((t, LANES), F32)],
        compiler_params=_cparams(1),
        name="ln1_router",
    )(merged, w_out, h, ln_g, ln_b, w_route, b_route)


def _moe_kernel(be_ref, nused_ref, gcur_ref, gnext_ref, sidx_ref, x_hbm, wg_ref, wu_ref, wd_ref,
                y_hbm, xbuf, ybuf, gsem, ssem, *, bm):
    del be_ref
    i = pl.program_id(0)
    n_used = nused_ref[0]
    slot = i % 2

    def gather_copy(idx_ref, j, s):
        return pltpu.make_async_copy(x_hbm.at[pl.ds(idx_ref[0, 0, j], 1), :],
                                     xbuf.at[s, pl.ds(j, 1), :], gsem.at[s])

    def scatter_copy(idx_ref, j, s):
        return pltpu.make_async_copy(ybuf.at[s, pl.ds(j, 1), :],
                                     y_hbm.at[pl.ds(idx_ref[0, 0, j], 1), :], ssem.at[s])

    def for_rows(fn):
        def body(j, carry):
            fn(j)
            return carry
        lax.fori_loop(0, bm, body, 0, unroll=8)

    @pl.when(i == 0)
    def _():
        ybuf[...] = jnp.zeros(ybuf.shape, F32)
        spare0 = y_hbm.shape[0] - 2 * bm
        for s in range(2):
            fill = pltpu.make_async_copy(ybuf.at[s], y_hbm.at[pl.ds(spare0 + s * bm, bm), :], ssem.at[s])
            fill.start()
            fill.wait()

    @pl.when(jnp.logical_and(i == 0, n_used > 0))
    def _():
        for_rows(lambda j: gather_copy(gcur_ref, j, 0).start())

    @pl.when(i + 1 < n_used)
    def _():
        for_rows(lambda j: gather_copy(gnext_ref, j, 1 - slot).start())

    @pl.when(i < n_used)
    def _():
        for_rows(lambda j: gather_copy(gcur_ref, j, slot).wait())
        x = xbuf[slot].astype(BF16)
        gate = jnp.dot(x, wg_ref[0], preferred_element_type=F32)
        up = jnp.dot(x, wu_ref[0], preferred_element_type=F32)
        hmid = (_silu(gate) * up).astype(BF16)
        ybuf[slot] = jnp.dot(hmid, wd_ref[0], preferred_element_type=F32)
        for_rows(lambda j: scatter_copy(sidx_ref, j, slot).start())

        @pl.when(i >= 1)
        def _():
            for_rows(lambda j: scatter_copy(sidx_ref, j, 1 - slot).wait())

        @pl.when(i == n_used - 1)
        def _():
            for_rows(lambda j: scatter_copy(sidx_ref, j, slot).wait())


def _moe_experts(x, ids, w_gate, w_up, w_down, *, bm):
    t = x.shape[0]
    n_assign = t * TOP_K
    n_blocks = -(-(n_assign + N_EXPERTS * (bm - 1)) // bm)
    rows = n_blocks * bm
    flat = ids.reshape(-1)
    order = jnp.argsort(flat, stable=True).astype(I32)
    sorted_e = flat[order]
    counts = jnp.bincount(flat, length=N_EXPERTS).astype(I32)
    starts = jnp.cumsum(counts) - counts
    padded = (counts + bm - 1) // bm * bm
    pad_ends = jnp.cumsum(padded)
    pad_starts = pad_ends - padded
    dest_sorted = pad_starts[sorted_e] + jnp.arange(n_assign, dtype=I32) - starts[sorted_e]
    row_assign = jnp.full((rows,), n_assign, I32).at[dest_sorted].set(order)
    n_used = (pad_ends[-1] // bm).astype(I32)
    blk = jnp.arange(n_blocks, dtype=I32)
    block_expert = jnp.minimum(jnp.searchsorted(pad_ends, blk * bm, side='right'), N_EXPERTS - 1).astype(I32)
    last_used = block_expert[jnp.maximum(n_used - 1, 0)]
    block_expert = jnp.where(blk < n_used, block_expert, last_used)
    valid = row_assign < n_assign
    pos = jnp.arange(rows, dtype=I32)
    gather_idx = jnp.where(valid, row_assign // TOP_K, 0).reshape(n_blocks, 1, bm)
    dump = n_assign + ((pos // bm) % 2) * bm + pos % bm
    scatter_idx = jnp.where(valid, row_assign, dump).reshape(n_blocks, 1, bm)

    def idx_spec(shift):
        return pl.BlockSpec((1, 1, bm), lambda i, be, nu: (jnp.minimum(i + shift, n_blocks - 1), 0, 0),
                            memory_space=pltpu.SMEM)

    def w_spec(shape):
        return pl.BlockSpec((1,) + shape, lambda i, be, nu: (be[i], 0, 0))

    grid_spec = pltpu.PrefetchScalarGridSpec(
        num_scalar_prefetch=2,
        grid=(n_blocks,),
        in_specs=[idx_spec(0), idx_spec(1), idx_spec(0),
                  pl.BlockSpec(memory_space=pl.ANY),
                  w_spec((D_MODEL, D_FF)), w_spec((D_MODEL, D_FF)), w_spec((D_FF, D_MODEL))],
        out_specs=pl.BlockSpec(memory_space=pl.ANY),
        scratch_shapes=[pltpu.VMEM((2, bm, D_MODEL), F32),
                        pltpu.VMEM((2, bm, D_MODEL), F32),
                        pltpu.SemaphoreType.DMA((2,)),
                        pltpu.SemaphoreType.DMA((2,))],
    )
    y = pl.pallas_call(
        functools.partial(_moe_kernel, bm=bm),
        grid_spec=grid_spec,
        out_shape=jax.ShapeDtypeStruct((n_assign + 2 * bm, D_MODEL), F32),
        compiler_params=_cparams(1),
        name="moe_experts",
    )(block_expert, n_used.reshape(1), gather_idx, gather_idx, scatter_idx, x, w_gate, w_up, w_down)
    return y.reshape((n_assign + 2 * bm) // TOP_K, TOP_K * D_MODEL)


def _final_kernel(h1_ref, y_ref, wts_ref, g_ref, b_ref, o_ref):
    ffn = jnp.zeros(h1_ref.shape, F32)
    for k in range(TOP_K):
        ffn = ffn + wts_ref[:, k:k + 1] * y_ref[:, k * D_MODEL:(k + 1) * D_MODEL]
    o_ref[...] = _layer_norm(ALPHA * h1_ref[...] + ffn, g_ref[...], b_ref[...])


def _final_norm(h1, y, wts, ln_g, ln_b, *, tm):
    t = h1.shape[0]

    def const(shape):
        return pl.BlockSpec(shape, lambda i: (0,) * len(shape))

    return pl.pallas_call(
        _final_kernel,
        grid=(t // tm,),
        in_specs=[pl.BlockSpec((tm, D_MODEL), lambda i: (i, 0)),
                  pl.BlockSpec((tm, TOP_K * D_MODEL), lambda i: (i, 0)),
                  pl.BlockSpec((tm, LANES), lambda i: (i, 0)),
                  const((1, D_MODEL)), const((1, D_MODEL))],
        out_specs=pl.BlockSpec((tm, D_MODEL), lambda i: (i, 0)),
        out_shape=jax.ShapeDtypeStruct((t, D_MODEL), F32),
        compiler_params=_cparams(1),
        name="final_norm",
    )(h1, y, wts, ln_g, ln_b)


def _layer_tail(h, attn_o, ssm_o, gates, p, *, tm):
    merged = _merge(attn_o, ssm_o, gates, p["w_branch_attn"], p["w_branch_ssm"], tm=tm, tn=1024)
    h1, ids, wts = _ln1_router(merged, p["w_out"], h, p["ln1_g"], p["ln1_b"],
                               p["w_route"], p["b_route"], tm=tm)
    y = _moe_experts(h1, ids[:, :TOP_K], p["w_gate"], p["w_up"], p["w_down"], bm=MOE_BLOCK)
    return _final_norm(h1, y, wts, p["ln2_g"], p["ln2_b"], tm=tm)


def kernel(x_prompt, x_sample, cache_k, cache_v, page_table, state_conv, state_ssm, rel_bias, w_in, lambda_q1, lambda_k1, lambda_q2, lambda_k2, attn_subln, conv_w, conv_b, dt_bias, a_log, d_skip, ssm_norm_w, w_branch_attn, w_branch_ssm, w_out, ln1_g, ln1_b, w_route_group, b_route_group, w_route_expert, b_route_expert, w_gate, w_up, w_down, ln2_g, ln2_b):
    bsz, seq, _ = x_prompt.shape
    n_dec = x_sample.shape[0]
    assert x_sample.shape[1] == 1
    n_pool, page = cache_k.shape[:2]

    w_main = w_in[:, :OFF_DT].astype(BF16)
    w_dt = jnp.pad(w_in[:, OFF_DT:OFF_GATES], ((0, 0), (0, LANES - SSM_HEADS))).astype(BF16)
    w_gates = w_in[:, OFF_GATES:].astype(BF16)
    lam_rows = jnp.stack([lambda_q1, lambda_k1, lambda_q2, lambda_k2]).astype(F32)
    subln = attn_subln.reshape(1, V_DIM)
    pad_heads = lambda vec: jnp.pad(vec.astype(F32), (0, LANES - SSM_HEADS))
    conv_b2 = conv_b.reshape(1, CONV_DIM)
    dskip_row = jnp.repeat(d_skip.astype(F32), SSM_HEAD_DIM).reshape(1, D_INNER)
    normw_row = ssm_norm_w.reshape(1, D_INNER)
    n_route = MOE_GROUPS + N_EXPERTS
    tail = dict(
        w_branch_attn=w_branch_attn.astype(BF16), w_branch_ssm=w_branch_ssm.astype(BF16),
        w_out=w_out.astype(BF16), ln1_g=ln1_g.reshape(1, D_MODEL), ln1_b=ln1_b.reshape(1, D_MODEL),
        w_route=jnp.pad(jnp.concatenate([w_route_group, w_route_expert], axis=1),
                        ((0, 0), (0, LANES - n_route))),
        b_route=jnp.pad(jnp.concatenate([b_route_group, b_route_expert]),
                        (0, LANES - n_route)).reshape(1, LANES),
        w_gate=w_gate.astype(BF16), w_up=w_up.astype(BF16), w_down=w_down.astype(BF16),
        ln2_g=ln2_g.reshape(1, D_MODEL), ln2_b=ln2_b.reshape(1, D_MODEL))

    t_p = bsz * seq
    xp = x_prompt.reshape(t_p, D_MODEL)
    q, k, k_bf, v, v_bf, z, xbc, dt, gates = _in_projection(xp.astype(BF16), w_main, w_dt, w_gates, tm=1024)
    attn_p = _prompt_attention(q.reshape(bsz, seq, ATTN_WIDTH), k_bf.reshape(bsz, seq, KV_WIDTH),
                               v_bf.reshape(bsz, seq, KV_WIDTH), rel_bias, lam_rows, subln, tq=512)
    xbc3 = xbc.reshape(bsz, seq, CONV_DIM)
    ssm_p, ssm_state_p = _ssd_prompt(
        xbc3, z.reshape(bsz, seq, D_INNER), dt.reshape(bsz, seq, LANES), conv_w, conv_b2,
        pad_heads(dt_bias).reshape(LANES, 1), pad_heads(a_log).reshape(LANES, 1), dskip_row, normw_row)
    y_prompt = _layer_tail(xp, attn_p.reshape(t_p, ATTN_WIDTH), ssm_p.reshape(t_p, D_INNER), gates,
                           tail, tm=512).reshape(bsz, seq, D_MODEL)
    k_prompt = k.reshape(bsz, seq, N_KV_HEADS, V_DIM)
    v_prompt = v.reshape(bsz, seq, N_KV_HEADS, V_DIM)
    conv_prompt = xbc3[:, seq - (CONV_WIDTH - 1):, :]
    ssm_prompt = ssm_state_p.reshape(bsz, SSM_HEADS, SSM_HEAD_DIM, D_STATE)

    xs = x_sample.reshape(n_dec, D_MODEL)
    q, k, k_bf, v, v_bf, z, xbc, dt, gates = _in_projection(xs.astype(BF16), w_main, w_dt, w_gates, tm=n_dec)
    attn_s = _decode_attention(q, k, v, cache_k.reshape(n_pool, page, KV_WIDTH),
                               cache_v.reshape(n_pool, page, KV_WIDTH), page_table, rel_bias,
                               lam_rows, subln, pages_per_step=8)
    ssm_s, conv_sample, ssm_state_s = _ssd_step(
        xbc, state_conv, z, dt, state_ssm.reshape(n_dec, D_INNER, D_STATE), conv_w, conv_b2,
        pad_heads(dt_bias).reshape(1, LANES), pad_heads(a_log).reshape(1, LANES), dskip_row, normw_row)
    y_sample = _layer_tail(xs, attn_s, ssm_s, gates, tail, tm=n_dec).reshape(n_dec, 1, D_MODEL)
    k_sample = k.reshape(n_dec, 1, N_KV_HEADS, V_DIM)
    v_sample = v.reshape(n_dec, 1, N_KV_HEADS, V_DIM)
    ssm_sample = ssm_state_s.reshape(n_dec, SSM_HEADS, SSM_HEAD_DIM, D_STATE)

    return (y_prompt, y_sample, k_prompt, v_prompt, conv_prompt, ssm_prompt,
            k_sample, v_sample, conv_sample, ssm_sample)
```

```python
import functools
import math

import jax
import jax.numpy as jnp
import numpy as np
from jax import lax
from jax.experimental import pallas as pl
from jax.experimental.pallas import tpu as pltpu

F32 = jnp.float32
BF16 = jnp.bfloat16
I32 = jnp.int32

D_MODEL = 2048
DEPTH = 1
N_HEADS = 8
N_KV_HEADS = 4
Q_PER_KV = N_HEADS // N_KV_HEADS
HEAD_DIM = D_MODEL // (2 * N_HEADS)
V_DIM = 2 * HEAD_DIM
ATTN_WIDTH = N_HEADS * V_DIM
KV_WIDTH = N_KV_HEADS * V_DIM
N_BUCKETS = 32
MAX_EXACT = N_BUCKETS // 2
MAX_DISTANCE = 128
D_INNER = 2 * D_MODEL
SSM_HEAD_DIM = 64
SSM_HEADS = D_INNER // SSM_HEAD_DIM
SSM_GROUPS = 8
SSM_HPG = SSM_HEADS // SSM_GROUPS
D_STATE = 128
CONV_WIDTH = 4
GN = SSM_GROUPS * D_STATE
CONV_DIM = D_INNER + 2 * GN
SSD_CHUNK = 128
GROUP_WIDTH = D_INNER // SSM_GROUPS
MOE_GROUPS = 4
EXPERTS_PER_GROUP = 8
N_EXPERTS = MOE_GROUPS * EXPERTS_PER_GROUP
TOP_K = 2
D_FF = D_MODEL // 2
MOE_BLOCK = 128
ALPHA = (2 * DEPTH) ** 0.25
EPS = 1e-5
LAM_INIT = 0.8 - 0.6 * math.exp(-0.3 * 0)
Q_SCALE = HEAD_DIM ** -0.5
LOG2E = math.log2(math.e)
OFF_Q, OFF_K, OFF_V, OFF_Z, OFF_XBC = 0, 2048, 3072, 4096, 8192
OFF_DT = OFF_XBC + CONV_DIM
OFF_GATES = OFF_DT + SSM_HEADS

LANES = 128
V7X_VMEM_LIMIT = 56 * 1024 * 1024


def _cparams(n_axes, vmem=V7X_VMEM_LIMIT):
    return pltpu.CompilerParams(
        dimension_semantics=("arbitrary",) * n_axes, vmem_limit_bytes=vmem)


def _sigmoid(x):
    return 1.0 / (1.0 + jnp.exp(-x))


def _silu(x):
    return x * _sigmoid(x)


def _softplus(x):
    return jnp.maximum(x, 0.0) + jnp.log(1.0 + jnp.exp(-jnp.abs(x)))


def _layer_norm(x, g, b):
    xc = x - jnp.mean(x, axis=-1, keepdims=True)
    var = jnp.mean(xc * xc, axis=-1, keepdims=True)
    return xc * lax.rsqrt(var + EPS) * g + b


def _lane_tile(x, n):
    return jnp.concatenate([x] * n, axis=1)


def _split3(v):
    hi = v.astype(BF16)
    r1 = v - hi.astype(F32)
    mid = r1.astype(BF16)
    lo = (r1 - mid.astype(F32)).astype(BF16)
    return jnp.concatenate([hi, mid, lo], axis=1)


def _proj_kernel(x_ref, w_ref, *o_refs, scale):
    acc = jnp.dot(x_ref[...], w_ref[...], preferred_element_type=F32)
    if scale != 1.0:
        acc = acc * scale
    for o_ref in o_refs:
        o_ref[...] = acc.astype(o_ref.dtype)


def _proj_heads_kernel(x_ref, w_ref, o_ref, obf_ref):
    acc = jnp.dot(x_ref[...], w_ref[...], preferred_element_type=F32)
    for g in range(N_KV_HEADS):
        o_ref[:, g, :] = acc[:, g * V_DIM:(g + 1) * V_DIM]
    obf_ref[...] = acc.astype(BF16)


def _proj_heads(x, w, col0, *, tm, name):
    t, k = x.shape
    assert col0 % KV_WIDTH == 0 and t % tm == 0
    off = col0 // KV_WIDTH
    return pl.pallas_call(
        _proj_heads_kernel,
        grid=(t // tm,),
        in_specs=[pl.BlockSpec((tm, k), lambda i: (i, 0)),
                  pl.BlockSpec((k, KV_WIDTH), lambda i: (0, off))],
        out_specs=[pl.BlockSpec((tm, N_KV_HEADS, V_DIM), lambda i: (i, 0, 0)),
                   pl.BlockSpec((tm, KV_WIDTH), lambda i: (i, 0))],
        out_shape=[jax.ShapeDtypeStruct((t, N_KV_HEADS, V_DIM), F32),
                   jax.ShapeDtypeStruct((t, KV_WIDTH), BF16)],
        compiler_params=_cparams(1),
        name=name,
    )(x, w)


def _proj(x, w, col0, ncols, out_dtypes, *, scale=1.0, tm, tn, name):
    t, k = x.shape
    off = col0 // tn
    assert col0 % tn == 0 and ncols % tn == 0 and t % tm == 0
    return pl.pallas_call(
        functools.partial(_proj_kernel, scale=scale),
        grid=(t // tm, ncols // tn),
        in_specs=[pl.BlockSpec((tm, k), lambda i, j: (i, 0)),
                  pl.BlockSpec((k, tn), lambda i, j: (0, j + off))],
        out_specs=[pl.BlockSpec((tm, tn), lambda i, j: (i, j)) for _ in out_dtypes],
        out_shape=[jax.ShapeDtypeStruct((t, ncols), dt) for dt in out_dtypes],
        compiler_params=_cparams(2),
        name=name,
    )(x, w)


def _in_projection(x_bf, w_main, w_dt, w_gates, tm, q_scale):
    tn = 1024
    (q,) = _proj(x_bf, w_main, OFF_Q, ATTN_WIDTH, [BF16], scale=q_scale, tm=tm, tn=tn, name="proj_q")
    k, k_bf = _proj_heads(x_bf, w_main, OFF_K, tm=tm, name="proj_k")
    v, v_bf = _proj_heads(x_bf, w_main, OFF_V, tm=tm, name="proj_v")
    (z,) = _proj(x_bf, w_main, OFF_Z, D_INNER, [F32], tm=tm, tn=tn, name="proj_z")
    (xbc,) = _proj(x_bf, w_main, OFF_XBC, CONV_DIM, [F32], tm=tm, tn=tn, name="proj_xbc")
    (dt,) = _proj(x_bf, w_dt, 0, LANES, [F32], tm=tm, tn=LANES, name="proj_dt")
    (gates,) = _proj(x_bf, w_gates, 0, 2 * D_MODEL, [F32], tm=tm, tn=tn, name="proj_gates")
    return q, k, k_bf, v, v_bf, z, xbc, dt, gates


def _t5_bucket(dist):
    n = jnp.maximum(dist, 0)
    nf = jnp.maximum(n, 1).astype(F32)
    large = MAX_EXACT + (jnp.log(nf / MAX_EXACT) / math.log(MAX_DISTANCE / MAX_EXACT)
                         * (N_BUCKETS - MAX_EXACT)).astype(I32)
    large = jnp.minimum(large, N_BUCKETS - 1)
    return jnp.where(n < MAX_EXACT, n, large)


def _bias_minus_far(dist, relb_ref, head):
    bucket = _t5_bucket(dist)
    far = relb_ref[N_BUCKETS - 1, head]
    out = jnp.zeros(dist.shape, F32)
    for b in range(N_BUCKETS - 1):
        out = jnp.where(bucket == b, relb_ref[b, head] - far, out)
    return out


def _lambda_value(lam_ref):
    s1 = jnp.sum(lam_ref[0:1, :] * lam_ref[1:2, :], axis=1, keepdims=True)
    s2 = jnp.sum(lam_ref[2:3, :] * lam_ref[3:4, :], axis=1, keepdims=True)
    return jnp.exp(s1) - jnp.exp(s2) + LAM_INIT


def _subln(o, subln_ref):
    ms = jnp.mean(o * o, axis=-1, keepdims=True)
    return o * lax.rsqrt(ms + EPS) * subln_ref[...] * (1.0 - LAM_INIT)


def _prompt_attn_kernel(relb_ref, lam_ref, subln_ref, q_ref, k_ref, v_ref, o_ref,
                        qs_ref, m_ref, l_ref, acc_ref, bias_ref, *, tq):
    g = pl.program_id(0)
    b = pl.program_id(1)
    qi = pl.program_id(2)
    rows = 2 * tq

    @pl.when(jnp.logical_and(b == 0, qi == 0))
    def _():
        r = lax.broadcasted_iota(I32, (tq, tq), 0)
        c = lax.broadcasted_iota(I32, (tq, tq), 1)
        for hh in range(Q_PER_KV):
            head = g * Q_PER_KV + hh
            adj = _bias_minus_far(tq + r - c, relb_ref, head) * LOG2E
            diag = _bias_minus_far(r - c, relb_ref, head) * LOG2E
            diag = jnp.where(r >= c, diag, -jnp.inf)
            bias_ref[0, hh * tq:(hh + 1) * tq, :] = adj
            bias_ref[1, hh * tq:(hh + 1) * tq, :] = diag

    for hh in range(Q_PER_KV):
        for mm in range(2):
            c0 = hh * V_DIM + mm * HEAD_DIM
            qs_ref[mm, hh * tq:(hh + 1) * tq, :] = q_ref[0, :, c0:c0 + HEAD_DIM]

    m_ref[...] = jnp.full(m_ref.shape, -jnp.inf, F32)
    l_ref[...] = jnp.zeros(l_ref.shape, F32)
    acc_ref[...] = jnp.zeros(acc_ref.shape, F32)

    chains = [(mm, hh) for mm in range(2) for hh in range(Q_PER_KV)]

    def key_tile_start(ki):
        return pl.multiple_of(ki * tq, tq)

    def chain_update(mm, hh, ki, bias_kind):
        rs = slice(hh * tq, (hh + 1) * tq)
        kt = k_ref[0, pl.ds(key_tile_start(ki), tq), mm * HEAD_DIM:(mm + 1) * HEAD_DIM]
        vt = v_ref[0, pl.ds(key_tile_start(ki), tq), :]
        s = lax.dot_general(qs_ref[mm, rs, :], kt, (((1,), (1,)), ((), ())),
                            preferred_element_type=F32)
        if bias_kind is not None:
            s = s + bias_ref[bias_kind, rs, :]
        m_prev = m_ref[mm, rs, :]
        m_new = jnp.maximum(m_prev, jnp.max(s, axis=1, keepdims=True))
        p = jnp.exp2(s - _lane_tile(m_new, tq // LANES))
        alpha = jnp.exp2(m_prev - m_new)
        l_ref[mm, rs, :] = alpha * l_ref[mm, rs, :] + jnp.sum(p, axis=1, keepdims=True)
        acc_ref[mm, rs, :] = (acc_ref[mm, rs, :] * _lane_tile(alpha, V_DIM // LANES)
                              + jnp.dot(p.astype(BF16), vt, preferred_element_type=F32))
        m_ref[mm, rs, :] = m_new

    def tile_update(ki, bias_kind):
        for mm, hh in chains:
            chain_update(mm, hh, ki, bias_kind)

    def far_body(ki, carry):
        tile_update(ki, None)
        return carry
    lax.fori_loop(0, qi - 1, far_body, 0)

    @pl.when(qi >= 1)
    def _():
        tile_update(qi - 1, 0)

    tile_update(qi, 1)

    lam = _lambda_value(lam_ref)
    o1 = acc_ref[0] / _lane_tile(l_ref[0], V_DIM // LANES)
    o2 = acc_ref[1] / _lane_tile(l_ref[1], V_DIM // LANES)
    o = _subln(o1 - lam * o2, subln_ref)
    for hh in range(Q_PER_KV):
        o_ref[0, :, hh * V_DIM:(hh + 1) * V_DIM] = o[hh * tq:(hh + 1) * tq].astype(o_ref.dtype)


def _prompt_attention(q, k_bf, v_bf, rel_bias, lam_rows, subln, *, tq):
    bsz, length, _ = q.shape
    assert length % tq == 0 and tq % LANES == 0
    gw = Q_PER_KV * V_DIM
    rows = 2 * tq
    return pl.pallas_call(
        functools.partial(_prompt_attn_kernel, tq=tq),
        grid=(N_KV_HEADS, bsz, length // tq),
        in_specs=[
            pl.BlockSpec(memory_space=pltpu.SMEM),
            pl.BlockSpec((4, HEAD_DIM), lambda g, b, i: (0, 0)),
            pl.BlockSpec((1, V_DIM), lambda g, b, i: (0, 0)),
            pl.BlockSpec((1, tq, gw), lambda g, b, i: (b, i, g)),
            pl.BlockSpec((1, length, V_DIM), lambda g, b, i: (b, 0, g)),
            pl.BlockSpec((1, length, V_DIM), lambda g, b, i: (b, 0, g)),
        ],
        out_specs=pl.BlockSpec((1, tq, gw), lambda g, b, i: (b, i, g)),
        out_shape=jax.ShapeDtypeStruct((bsz, length, ATTN_WIDTH), BF16),
        scratch_shapes=[
            pltpu.VMEM((2, rows, HEAD_DIM), BF16),
            pltpu.VMEM((2, rows, LANES), F32),
            pltpu.VMEM((2, rows, LANES), F32),
            pltpu.VMEM((2, rows, V_DIM), F32),
            pltpu.VMEM((2, rows, tq), F32),
        ],
        compiler_params=_cparams(3),
        name="prompt_attention",
    )(rel_bias, lam_rows, subln, q, k_bf, v_bf)


def _decode_attn_kernel(pt_ref, relb_ref, lam_ref, subln_ref, q_ref, kn_ref, vn_ref, *rest,
                        pages_per_step, page, past_len):
    del pt_ref
    kp = rest[:pages_per_step]
    vp = rest[pages_per_step:2 * pages_per_step]
    o_ref, m_ref, l_ref, acc_ref = rest[2 * pages_per_step:]
    j = pl.program_id(1)
    n_steps = pl.num_programs(1)
    rpt = 2 * N_KV_HEADS
    prow = page * rpt
    nq = 2 * N_HEADS

    @pl.when(j == 0)
    def _():
        m_ref[...] = jnp.full(m_ref.shape, -jnp.inf, F32)
        l_ref[...] = jnp.zeros(l_ref.shape, F32)
        acc_ref[...] = jnp.zeros(acc_ref.shape, F32)

    q_all = q_ref[0]

    def row_head(shape):
        r = lax.broadcasted_iota(I32, shape, 0)
        return (r % N_KV_HEADS) * Q_PER_KV + (r // N_KV_HEADS) % Q_PER_KV

    def row_bias(dist):
        out = jnp.zeros(dist.shape, F32)
        head = row_head(dist.shape)
        for h in range(N_HEADS):
            out = jnp.where(head == h, _bias_minus_far(dist, relb_ref, h), out)
        return out

    def online(s, pv_fn):
        m_prev = m_ref[...]
        m_new = jnp.maximum(m_prev, jnp.max(s, axis=1, keepdims=True))
        p = jnp.exp(s - m_new[:, :1])
        alpha = jnp.exp(m_prev - m_new)
        l_ref[...] = alpha * l_ref[...] + jnp.sum(p, axis=1, keepdims=True)
        acc_ref[...] = acc_ref[...] * jnp.concatenate([alpha, alpha], axis=0) + pv_fn(p)
        m_ref[...] = m_new

    s_parts = [lax.dot_general(q_all, kp[t][0].astype(BF16), (((1,), (1,)), ((), ())),
                               preferred_element_type=F32) for t in range(pages_per_step)]

    def add_near_bias(s_last):
        c = lax.broadcasted_iota(I32, (nq, prow), 1)
        pos = (j * pages_per_step + pages_per_step - 1) * page + c // rpt
        return s_last + row_bias(past_len - pos)

    s_parts[-1] = lax.cond(j == n_steps - 1, add_near_bias, lambda s: s, s_parts[-1])
    s = jnp.concatenate(s_parts, axis=1)
    col = lax.broadcasted_iota(I32, s.shape, 1)
    row = lax.broadcasted_iota(I32, s.shape, 0)
    own = col % rpt == (row // (nq // 2)) * N_KV_HEADS + row % N_KV_HEADS
    s = jnp.where(own, s, -jnp.inf)

    def pv(p):
        width = p.shape[1]
        up = pltpu.roll(p, N_KV_HEADS, axis=1)
        down = pltpu.roll(p, width - N_KV_HEADS, axis=1)
        top = row[:, :1] < nq // 2
        p_half0 = jnp.where(top, p, down)
        p_half1 = jnp.where(top, up, p)
        lhs = jnp.concatenate([p_half0, p_half1], axis=0).astype(BF16)
        out = jnp.zeros((2 * nq, LANES), F32)
        for t in range(pages_per_step):
            out = out + jnp.dot(lhs[:, t * prow:(t + 1) * prow], vp[t][0].astype(BF16),
                                preferred_element_type=F32)
        return out

    online(s, pv)

    @pl.when(j == n_steps - 1)
    def _():
        def per_row(ref, part):
            blk = ref[0, :, part * LANES:(part + 1) * LANES].astype(BF16).astype(F32)
            return jnp.concatenate([blk] * Q_PER_KV, axis=0)

        kn = jnp.concatenate([per_row(kn_ref, 0), per_row(kn_ref, 1)], axis=0)
        s_new = jnp.sum(q_all.astype(F32) * kn, axis=1, keepdims=True)
        s_new = s_new + row_bias(jnp.zeros((nq, 1), I32))
        vn = jnp.concatenate([per_row(vn_ref, 0)] * 2 + [per_row(vn_ref, 1)] * 2, axis=0)
        online(s_new, lambda p: jnp.concatenate([p, p], axis=0).astype(BF16).astype(F32) * vn)

        lam = _lambda_value(lam_ref)
        l_all = l_ref[...]
        o_lo = acc_ref[0:nq] / l_all
        o_hi = acc_ref[nq:2 * nq] / l_all
        o_all = jnp.concatenate([o_lo, o_hi], axis=1)
        o = _subln(o_all[0:nq // 2] - lam * o_all[nq // 2:nq], subln_ref)
        for hh in range(Q_PER_KV):
            for g in range(N_KV_HEADS):
                h = g * Q_PER_KV + hh
                r = hh * N_KV_HEADS + g
                o_ref[0, :, h * V_DIM:(h + 1) * V_DIM] = o[r:r + 1].astype(o_ref.dtype)


def _decode_attention(q, k_new, v_new, cache_k, cache_v, page_table, rel_bias, lam_rows, subln,
                      *, pages_per_step):
    n_seq = q.shape[0]
    page = cache_k.shape[1]
    n_pages = page_table.shape[1]
    assert n_pages % pages_per_step == 0
    past_len = n_pages * page
    q3 = (q.reshape(n_seq, N_KV_HEADS, Q_PER_KV, 2, HEAD_DIM).transpose(0, 3, 2, 1, 4)
          .reshape(n_seq, N_HEADS * 2, HEAD_DIM))
    prow = page * 2 * N_KV_HEADS

    def stored_rows(cache):
        return (cache.reshape(-1, page, N_KV_HEADS, 2, LANES).transpose(0, 1, 3, 2, 4)
                .reshape(-1, prow, LANES))

    def page_spec(t):
        return pl.BlockSpec((1, prow, LANES),
                            lambda i, j, pt, t=t: (pt[i, j * pages_per_step + t], 0, 0))

    grid_spec = pltpu.PrefetchScalarGridSpec(
        num_scalar_prefetch=1,
        grid=(n_seq, n_pages // pages_per_step),
        in_specs=[
            pl.BlockSpec(memory_space=pltpu.SMEM),
            pl.BlockSpec((4, HEAD_DIM), lambda i, j, pt: (0, 0)),
            pl.BlockSpec((1, V_DIM), lambda i, j, pt: (0, 0)),
            pl.BlockSpec((1, N_HEADS * 2, HEAD_DIM), lambda i, j, pt: (i, 0, 0)),
            pl.BlockSpec((1, N_KV_HEADS, V_DIM), lambda i, j, pt: (i, 0, 0)),
            pl.BlockSpec((1, N_KV_HEADS, V_DIM), lambda i, j, pt: (i, 0, 0)),
        ] + [page_spec(t) for t in range(pages_per_step)] * 2,
        out_specs=pl.BlockSpec((1, 1, ATTN_WIDTH), lambda i, j, pt: (i, 0, 0)),
        scratch_shapes=[
            pltpu.VMEM((2 * N_HEADS, LANES), F32),
            pltpu.VMEM((2 * N_HEADS, LANES), F32),
            pltpu.VMEM((4 * N_HEADS, LANES), F32),
        ],
    )
    out = pl.pallas_call(
        functools.partial(_decode_attn_kernel, pages_per_step=pages_per_step, page=page,
                          past_len=past_len),
        grid_spec=grid_spec,
        out_shape=jax.ShapeDtypeStruct((n_seq, 1, ATTN_WIDTH), BF16),
        compiler_params=_cparams(2),
        name="decode_attention",
    )(page_table, rel_bias, lam_rows, subln, q3, k_new, v_new,
      *([stored_rows(cache_k)] * pages_per_step), *([stored_rows(cache_v)] * pages_per_step))
    return out.reshape(n_seq, ATTN_WIDTH)


def _gate_and_norm(y, z, normw):
    y = y * _silu(z)
    parts = []
    for g in range(SSM_GROUPS):
        yg = y[:, g * GROUP_WIDTH:(g + 1) * GROUP_WIDTH]
        ms = jnp.mean(yg * yg, axis=-1, keepdims=True)
        parts.append(yg * lax.rsqrt(ms + EPS))
    return jnp.concatenate(parts, axis=1) * normw


def _ssd_prompt_kernel(xbc_ref, z_ref, dt_ref, convw_ref, convb_ref, dtb_ref, alog_ref,
                       dskip_ref, normw_ref, o_ref, state_ref,
                       ext_ref, act_ref, st_ref, y_ref, e3_ref, *, chunk):
    c = pl.program_id(1)
    n_chunks = pl.num_programs(1)
    halo = CONV_WIDTH - 1
    base = 8
    pair_w = 2 * SSM_HEAD_DIM

    @pl.when(c == 0)
    def _():
        ext_ref[0:base, :] = jnp.zeros((base, CONV_DIM), F32)
        st_ref[...] = jnp.zeros(st_ref.shape, F32)
        r = lax.broadcasted_iota(I32, e3_ref.shape, 0) % LANES
        col = lax.broadcasted_iota(I32, e3_ref.shape, 1)
        e3_ref[...] = jnp.where(col // SSM_HEAD_DIM == r, 1.0, 0.0).astype(BF16)

    ext_ref[base:base + chunk, :] = xbc_ref[0]
    col_tile = 512
    for j in range(CONV_DIM // col_tile):
        cs = slice(j * col_tile, (j + 1) * col_tile)
        conv = convb_ref[:, cs] + jnp.zeros((chunk, col_tile), F32)
        for w in range(CONV_WIDTH):
            lo = base - halo + w
            conv = conv + ext_ref[lo:lo + chunk, cs] * convw_ref[w:w + 1, cs]
        act_ref[:, cs] = _silu(conv)
    ext_ref[base - halo:base, :] = ext_ref[base + chunk - halo:base + chunk, :]

    dt_t = _softplus(jnp.transpose(dt_ref[0]) + dtb_ref[...])
    a_t = dt_t * (-jnp.exp(alog_ref[...]))
    tt = lax.broadcasted_iota(I32, (chunk, chunk), 0)
    ss = lax.broadcasted_iota(I32, (chunk, chunk), 1)
    upper = jnp.where(tt <= ss, 1.0, 0.0).astype(BF16)
    upper3 = jnp.concatenate([upper, upper, upper], axis=0)
    acs_t = jnp.dot(_split3(a_t), upper3, preferred_element_type=F32)
    acs = jnp.transpose(acs_t)
    w_t = dt_t * jnp.exp(acs_t[:, chunk - 1:chunk] - acs_t)
    eexp = jnp.dot(_split3(jnp.exp(acs)), e3_ref[...],
                   preferred_element_type=F32)
    causal = tt >= ss

    for g in range(SSM_GROUPS):
        bg = act_ref[:, D_INNER + g * D_STATE:D_INNER + (g + 1) * D_STATE]
        cg = act_ref[:, D_INNER + GN + g * D_STATE:D_INNER + GN + (g + 1) * D_STATE]
        bg_bf = bg.astype(BF16)
        cg_bf = cg.astype(BF16)
        cb = lax.dot_general(cg_bf, bg_bf, (((1,), (1,)), ((), ())),
                             preferred_element_type=F32)
        bg_t = jnp.transpose(bg)
        gsl = slice(g * GROUP_WIDTH, (g + 1) * GROUP_WIDTH)
        y_off = jnp.dot(cg_bf, st_ref[:, gsl].astype(BF16), preferred_element_type=F32)
        for pr in range(SSM_HPG // 2):
            h0 = g * SSM_HPG + 2 * pr
            psl = slice(h0 * SSM_HEAD_DIM, h0 * SSM_HEAD_DIM + pair_w)
            lhs_d, lhs_s = [], []
            for h in (h0, h0 + 1):
                seg = acs[:, h:h + 1] - acs_t[h:h + 1, :]
                dec = jnp.exp(jnp.where(causal, seg, -jnp.inf))
                lhs_d.append((cb * dec * dt_t[h:h + 1, :]).astype(BF16))
                lhs_s.append((bg_t * w_t[h:h + 1, :]).astype(BF16))
            lhs = jnp.concatenate([jnp.concatenate(lhs_d, axis=1),
                                   jnp.concatenate(lhs_s, axis=1)], axis=0)
            xs = act_ref[:, psl]
            lane = lax.broadcasted_iota(I32, xs.shape, 1)
            rhs = jnp.concatenate([jnp.where(lane < SSM_HEAD_DIM, xs, 0.0),
                                   jnp.where(lane >= SSM_HEAD_DIM, xs, 0.0)], axis=0).astype(BF16)
            res = jnp.dot(lhs, rhs, preferred_element_type=F32)
            y_ref[:, psl] = (res[:chunk] + y_off[:, 2 * pr * SSM_HEAD_DIM:2 * pr * SSM_HEAD_DIM + pair_w]
                             * eexp[:, psl] + dskip_ref[:, psl] * xs)
            st_ref[:, psl] = st_ref[:, psl] * eexp[chunk - 1:chunk, psl] + res[chunk:]

    o_ref[0] = _gate_and_norm(y_ref[...], z_ref[0], normw_ref[...]).astype(o_ref.dtype)

    @pl.when(c == n_chunks - 1)
    def _():
        for j in range(D_INNER // LANES):
            state_ref[0, j * LANES:(j + 1) * LANES, :] = jnp.transpose(
                st_ref[:, j * LANES:(j + 1) * LANES])


def _ssd_prompt(xbc, z, dt, conv_w, conv_b, dtb_col, alog_col, dskip_row, norm_w):
    bsz, length, _ = xbc.shape
    chunk = SSD_CHUNK
    assert length % chunk == 0

    def const(shape):
        return pl.BlockSpec(shape, lambda b, c: (0,) * len(shape))

    return pl.pallas_call(
        functools.partial(_ssd_prompt_kernel, chunk=chunk),
        grid=(bsz, length // chunk),
        in_specs=[
            pl.BlockSpec((1, chunk, CONV_DIM), lambda b, c: (b, c, 0)),
            pl.BlockSpec((1, chunk, D_INNER), lambda b, c: (b, c, 0)),
            pl.BlockSpec((1, chunk, LANES), lambda b, c: (b, c, 0)),
            const((CONV_WIDTH, CONV_DIM)), const((1, CONV_DIM)),
            const((LANES, 1)), const((LANES, 1)),
            const((1, D_INNER)), const((1, D_INNER)),
        ],
        out_specs=[
            pl.BlockSpec((1, chunk, D_INNER), lambda b, c: (b, c, 0)),
            pl.BlockSpec((1, D_INNER, D_STATE), lambda b, c: (b, 0, 0)),
        ],
        out_shape=[
            jax.ShapeDtypeStruct((bsz, length, D_INNER), BF16),
            jax.ShapeDtypeStruct((bsz, D_INNER, D_STATE), F32),
        ],
        scratch_shapes=[
            pltpu.VMEM((8 + chunk, CONV_DIM), F32),
            pltpu.VMEM((chunk, CONV_DIM), F32),
            pltpu.VMEM((D_STATE, D_INNER), F32),
            pltpu.VMEM((chunk, D_INNER), F32),
            pltpu.VMEM((3 * LANES, D_INNER), BF16),
        ],
        compiler_params=_cparams(2),
        name="ssd_prompt",
    )(xbc, z, dt, conv_w, conv_b, dtb_col, alog_col, dskip_row, norm_w)


def _row_to_col(row):
    r = lax.broadcasted_iota(I32, (LANES, LANES), 0)
    c = lax.broadcasted_iota(I32, (LANES, LANES), 1)
    return jnp.sum(jnp.where(r == c, jnp.broadcast_to(row, (LANES, LANES)), 0.0),
                   axis=1, keepdims=True)


def _ssd_step_kernel(xbc_ref, sconv_ref, z_ref, dt_ref, state_ref, convw_ref, convb_ref,
                     dtb_ref, alog_ref, dskip_ref, normw_ref,
                     o_ref, conv_out_ref, state_out_ref, e3_ref):
    halo = CONV_WIDTH - 1

    @pl.when(pl.program_id(0) == 0)
    def _():
        r = lax.broadcasted_iota(I32, e3_ref.shape, 0) % LANES
        col = lax.broadcasted_iota(I32, e3_ref.shape, 1)
        e3_ref[...] = jnp.where(col // SSM_HEAD_DIM == r, 1.0, 0.0).astype(BF16)

    x_new = xbc_ref[0]
    conv = convb_ref[...] + x_new * convw_ref[halo:halo + 1, :]
    for w in range(halo):
        conv = conv + sconv_ref[0, w:w + 1, :] * convw_ref[w:w + 1, :]
    act = _silu(conv)
    conv_out_ref[0, 0:halo - 1, :] = sconv_ref[0, 1:halo, :]
    conv_out_ref[0, halo - 1:halo, :] = x_new

    dt = _softplus(dt_ref[0] + dtb_ref[...])
    decay = jnp.exp(dt * (-jnp.exp(alog_ref[...])))
    both = jnp.concatenate([decay, dt, jnp.zeros((6, LANES), F32)], axis=0)
    expanded = jnp.dot(_split3(both), e3_ref[...], preferred_element_type=F32)
    decay_x = expanded[0:1]
    xs = act[:, :D_INNER]
    xdt = xs * expanded[1:2]

    y_parts = []
    for g in range(SSM_GROUPS):
        b_row = act[:, D_INNER + g * D_STATE:D_INNER + (g + 1) * D_STATE]
        c_row = act[:, D_INNER + GN + g * D_STATE:D_INNER + GN + (g + 1) * D_STATE]
        for k in range(GROUP_WIDTH // LANES):
            r0 = g * GROUP_WIDTH + k * LANES
            new = (state_ref[0, r0:r0 + LANES, :] * _row_to_col(decay_x[:, r0:r0 + LANES])
                   + _row_to_col(xdt[:, r0:r0 + LANES]) * b_row)
            state_out_ref[0, r0:r0 + LANES, :] = new
        h_new = state_out_ref[0, g * GROUP_WIDTH:(g + 1) * GROUP_WIDTH, :].astype(BF16)
        c8 = jnp.broadcast_to(c_row, (8, D_STATE)).astype(BF16)
        y_parts.append(lax.dot_general(c8, h_new, (((1,), (1,)), ((), ())),
                                       preferred_element_type=F32)[0:1])
    y = jnp.concatenate(y_parts, axis=1) + dskip_ref[...] * xs
    o_ref[0] = _gate_and_norm(y, z_ref[0], normw_ref[...]).astype(o_ref.dtype)


def _ssd_step(xbc, state_conv, z, dt, state_ssm, conv_w, conv_b, dtb_row, alog_row, dskip_row, norm_w):
    n_seq = xbc.shape[0]
    halo = CONV_WIDTH - 1

    def const(shape):
        return pl.BlockSpec(shape, lambda i: (0,) * len(shape))

    y, conv_out, state_out = pl.pallas_call(
        _ssd_step_kernel,
        grid=(n_seq,),
        in_specs=[
            pl.BlockSpec((1, 1, CONV_DIM), lambda i: (i, 0, 0)),
            pl.BlockSpec((1, halo, CONV_DIM), lambda i: (i, 0, 0)),
            pl.BlockSpec((1, 1, D_INNER), lambda i: (i, 0, 0)),
            pl.BlockSpec((1, 1, LANES), lambda i: (i, 0, 0)),
            pl.BlockSpec((1, D_INNER, D_STATE), lambda i: (i, 0, 0)),
            const((CONV_WIDTH, CONV_DIM)), const((1, CONV_DIM)),
            const((1, LANES)), const((1, LANES)),
            const((1, D_INNER)), const((1, D_INNER)),
        ],
        out_specs=[
            pl.BlockSpec((1, 1, D_INNER), lambda i: (i, 0, 0)),
            pl.BlockSpec((1, halo, CONV_DIM), lambda i: (i, 0, 0)),
            pl.BlockSpec((1, D_INNER, D_STATE), lambda i: (i, 0, 0)),
        ],
        out_shape=[
            jax.ShapeDtypeStruct((n_seq, 1, D_INNER), BF16),
            jax.ShapeDtypeStruct((n_seq, halo, CONV_DIM), F32),
            jax.ShapeDtypeStruct((n_seq, D_INNER, D_STATE), F32),
        ],
        scratch_shapes=[pltpu.VMEM((3 * LANES, D_INNER), BF16)],
        compiler_params=_cparams(1),
        name="ssd_step",
    )(xbc.reshape(n_seq, 1, CONV_DIM), state_conv, z.reshape(n_seq, 1, D_INNER),
      dt.reshape(n_seq, 1, LANES), state_ssm, conv_w, conv_b, dtb_row, alog_row, dskip_row, norm_w)
    return y.reshape(n_seq, D_INNER), conv_out, state_out


def _merge_kernel(attn_ref, ssm_ref, wa_ref, ws_ref, ga_ref, gb_ref, o_ref):
    a = jnp.dot(attn_ref[...], wa_ref[...], preferred_element_type=F32)
    s = jnp.dot(ssm_ref[...], ws_ref[...], preferred_element_type=F32)
    o_ref[...] = (_sigmoid(ga_ref[...]) * a + _sigmoid(gb_ref[...]) * s).astype(o_ref.dtype)


def _merge(attn_o, ssm_o, gates, w_attn, w_ssm, *, tm, tn):
    t = attn_o.shape[0]
    nb = D_MODEL // tn
    return pl.pallas_call(
        _merge_kernel,
        grid=(t // tm, nb),
        in_specs=[
            pl.BlockSpec((tm, ATTN_WIDTH), lambda i, j: (i, 0)),
            pl.BlockSpec((tm, D_INNER), lambda i, j: (i, 0)),
            pl.BlockSpec((ATTN_WIDTH, tn), lambda i, j: (0, j)),
            pl.BlockSpec((D_INNER, tn), lambda i, j: (0, j)),
            pl.BlockSpec((tm, tn), lambda i, j: (i, j)),
            pl.BlockSpec((tm, tn), lambda i, j: (i, j + nb)),
        ],
        out_specs=pl.BlockSpec((tm, tn), lambda i, j: (i, j)),
        out_shape=jax.ShapeDtypeStruct((t, D_MODEL), BF16),
        compiler_params=_cparams(2),
        name="branch_merge",
    )(attn_o, ssm_o, w_attn, w_ssm, gates, gates)


def _ln1_router_kernel(m_ref, wout_ref, h_ref, g_ref, b_ref, wr_ref, br_ref,
                       h1_ref, ids_ref, wts_ref):
    x = ALPHA * h_ref[...] + jnp.dot(m_ref[...], wout_ref[...], preferred_element_type=F32)
    h1 = _layer_norm(x, g_ref[...], b_ref[...])
    h1_ref[...] = h1

    logits = jnp.dot(h1, wr_ref[...], preferred_element_type=F32,
                     precision=lax.Precision.HIGHEST) + br_ref[...]
    lane = lax.broadcasted_iota(I32, logits.shape, 1)
    big = jnp.int32(LANES)
    neg = -jnp.inf
    gl = jnp.where(lane < MOE_GROUPS, logits, neg)
    gmax = jnp.max(gl, axis=1, keepdims=True)
    g_idx = jnp.min(jnp.where(gl == gmax, lane, big), axis=1, keepdims=True)
    g_w = 1.0 / jnp.sum(jnp.exp(gl - gmax), axis=1, keepdims=True)
    e_lo = MOE_GROUPS + g_idx * EXPERTS_PER_GROUP
    el = jnp.where(jnp.logical_and(lane >= e_lo, lane < e_lo + EXPERTS_PER_GROUP), logits, neg)
    e1 = jnp.max(el, axis=1, keepdims=True)
    i1 = jnp.min(jnp.where(el == e1, lane, big), axis=1, keepdims=True)
    el2 = jnp.where(lane == i1, neg, el)
    e2 = jnp.max(el2, axis=1, keepdims=True)
    i2 = jnp.min(jnp.where(el2 == e2, lane, big), axis=1, keepdims=True)
    t2 = jnp.exp(e2 - e1)
    den = 1.0 + t2
    w1 = g_w * (1.0 / den)
    w2 = g_w * (t2 / den)
    ids_ref[...] = jnp.where(lane == 0, i1 - MOE_GROUPS, jnp.where(lane == 1, i2 - MOE_GROUPS, 0))
    wts_ref[...] = jnp.where(lane == 0, w1, jnp.where(lane == 1, w2, 0.0))


def _ln1_router(merged, w_out, h, ln_g, ln_b, w_route, b_route, *, tm):
    t = merged.shape[0]

    def const(shape):
        return pl.BlockSpec(shape, lambda i: (0,) * len(shape))

    def rows(width):
        return pl.BlockSpec((tm, width), lambda i: (i, 0))

    return pl.pallas_call(
        _ln1_router_kernel,
        grid=(t // tm,),
        in_specs=[rows(D_MODEL), const((D_MODEL, D_MODEL)), rows(D_MODEL),
                  const((1, D_MODEL)), const((1, D_MODEL)),
                  const((D_MODEL, LANES)), const((1, LANES))],
        out_specs=[rows(D_MODEL), rows(LANES), rows(LANES)],
        out_shape=[jax.ShapeDtypeStruct((t, D_MODEL), F32),
                   jax.ShapeDtypeStruct((t, LANES), I32),
                   jax.ShapeDtypeStruct((t, LANES), F32)],
        compiler_params=_cparams(1),
        name="ln1_router",
    )(merged, w_out, h, ln_g, ln_b, w_route, b_route)


def _moe_kernel(be_ref, nused_ref, gcur_ref, gnext_ref, sidx_ref, x_hbm, wg_ref, wu_ref, wd_ref,
                y_hbm, xbuf, ybuf, gsem, ssem, *, bm):
    del be_ref
    i = pl.program_id(0)
    n_used = nused_ref[0]
    slot = i % 2

    def gather_copy(idx_ref, j, s):
        return pltpu.make_async_copy(x_hbm.at[pl.ds(idx_ref[0, 0, j], 1), :],
                                     xbuf.at[s, pl.ds(j, 1), :], gsem.at[s])

    def scatter_copy(idx_ref, j, s):
        return pltpu.make_async_copy(ybuf.at[s, pl.ds(j, 1), :],
                                     y_hbm.at[pl.ds(idx_ref[0, 0, j], 1), :], ssem.at[s])

    def for_rows(fn):
        def body(j, carry):
            fn(j)
            return carry
        lax.fori_loop(0, bm, body, 0, unroll=8)

    @pl.when(i == 0)
    def _():
        ybuf[...] = jnp.zeros(ybuf.shape, F32)
        slab = y_hbm.shape[0] // TOP_K
        for s in range(2):
            fill = pltpu.make_async_copy(
                ybuf.at[s], y_hbm.at[pl.ds((s + 1) * slab - bm, bm), :], ssem.at[s])
            fill.start()
            fill.wait()

    @pl.when(jnp.logical_and(i == 0, n_used > 0))
    def _():
        for_rows(lambda j: gather_copy(gcur_ref, j, 0).start())

    @pl.when(i + 1 < n_used)
    def _():
        for_rows(lambda j: gather_copy(gnext_ref, j, 1 - slot).start())

    @pl.when(i < n_used)
    def _():
        for_rows(lambda j: gather_copy(gcur_ref, j, slot).wait())
        x = xbuf[slot].astype(BF16)
        gate = jnp.dot(x, wg_ref[0], preferred_element_type=F32)
        up = jnp.dot(x, wu_ref[0], preferred_element_type=F32)
        hmid = (_silu(gate) * up).astype(BF16)
        ybuf[slot] = jnp.dot(hmid, wd_ref[0], preferred_element_type=F32)
        for_rows(lambda j: scatter_copy(sidx_ref, j, slot).start())

        @pl.when(i >= 1)
        def _():
            for_rows(lambda j: scatter_copy(sidx_ref, j, 1 - slot).wait())

        @pl.when(i == n_used - 1)
        def _():
            for_rows(lambda j: scatter_copy(sidx_ref, j, slot).wait())


def _moe_experts(x, ids, w_gate, w_up, w_down, *, bm):
    t = x.shape[0]
    n_assign = t * TOP_K
    n_blocks = -(-(n_assign + N_EXPERTS * (bm - 1)) // bm)
    rows = n_blocks * bm
    flat = ids.reshape(-1)
    order = jnp.argsort(flat, stable=True).astype(I32)
    sorted_e = flat[order]
    counts = jnp.bincount(flat, length=N_EXPERTS).astype(I32)
    starts = jnp.cumsum(counts) - counts
    padded = (counts + bm - 1) // bm * bm
    pad_ends = jnp.cumsum(padded)
    pad_starts = pad_ends - padded
    dest_sorted = pad_starts[sorted_e] + jnp.arange(n_assign, dtype=I32) - starts[sorted_e]
    row_assign = jnp.full((rows,), n_assign, I32).at[dest_sorted].set(order)
    n_used = (pad_ends[-1] // bm).astype(I32)
    blk = jnp.arange(n_blocks, dtype=I32)
    block_expert = jnp.minimum(jnp.searchsorted(pad_ends, blk * bm, side='right'), N_EXPERTS - 1).astype(I32)
    last_used = block_expert[jnp.maximum(n_used - 1, 0)]
    block_expert = jnp.where(blk < n_used, block_expert, last_used)
    valid = row_assign < n_assign
    pos = jnp.arange(rows, dtype=I32)
    gather_idx = jnp.where(valid, row_assign // TOP_K, 0).reshape(n_blocks, 1, bm)
    assert TOP_K == 2
    slab = t + bm
    dump = ((pos // bm) % 2) * slab + t + pos % bm
    dest = (row_assign % TOP_K) * slab + row_assign // TOP_K
    scatter_idx = jnp.where(valid, dest, dump).reshape(n_blocks, 1, bm)

    def idx_spec(shift):
        return pl.BlockSpec((1, 1, bm), lambda i, be, nu: (jnp.minimum(i + shift, n_blocks - 1), 0, 0),
                            memory_space=pltpu.SMEM)

    def w_spec(shape):
        return pl.BlockSpec((1,) + shape, lambda i, be, nu: (be[i], 0, 0))

    grid_spec = pltpu.PrefetchScalarGridSpec(
        num_scalar_prefetch=2,
        grid=(n_blocks,),
        in_specs=[idx_spec(0), idx_spec(1), idx_spec(0),
                  pl.BlockSpec(memory_space=pl.ANY),
                  w_spec((D_MODEL, D_FF)), w_spec((D_MODEL, D_FF)), w_spec((D_FF, D_MODEL))],
        out_specs=pl.BlockSpec(memory_space=pl.ANY),
        scratch_shapes=[pltpu.VMEM((2, bm, D_MODEL), F32),
                        pltpu.VMEM((2, bm, D_MODEL), F32),
                        pltpu.SemaphoreType.DMA((2,)),
                        pltpu.SemaphoreType.DMA((2,))],
    )
    y = pl.pallas_call(
        functools.partial(_moe_kernel, bm=bm),
        grid_spec=grid_spec,
        out_shape=jax.ShapeDtypeStruct((TOP_K * slab, D_MODEL), F32),
        compiler_params=_cparams(1),
        name="moe_experts",
    )(block_expert, n_used.reshape(1), gather_idx, gather_idx, scatter_idx, x, w_gate, w_up, w_down)
    return y.reshape(TOP_K, slab, D_MODEL)


def _final_kernel(h1_ref, *rest):
    y_refs = rest[:TOP_K]
    wts_ref, g_ref, b_ref, o_ref = rest[TOP_K:]
    ffn = jnp.zeros(h1_ref.shape, F32)
    for k in range(TOP_K):
        ffn = ffn + wts_ref[:, k:k + 1] * y_refs[k][0]
    o_ref[...] = _layer_norm(ALPHA * h1_ref[...] + ffn, g_ref[...], b_ref[...])


def _final_norm(h1, y, wts, ln_g, ln_b, *, tm):
    t = h1.shape[0]

    def const(shape):
        return pl.BlockSpec(shape, lambda i: (0,) * len(shape))

    return pl.pallas_call(
        _final_kernel,
        grid=(t // tm,),
        in_specs=[pl.BlockSpec((tm, D_MODEL), lambda i: (i, 0))]
        + [pl.BlockSpec((1, tm, D_MODEL), lambda i, k=k: (k, i, 0)) for k in range(TOP_K)]
        + [pl.BlockSpec((tm, LANES), lambda i: (i, 0)),
           const((1, D_MODEL)), const((1, D_MODEL))],
        out_specs=pl.BlockSpec((tm, D_MODEL), lambda i: (i, 0)),
        out_shape=jax.ShapeDtypeStruct((t, D_MODEL), F32),
        compiler_params=_cparams(1),
        name="final_norm",
    )(h1, *([y] * TOP_K), wts, ln_g, ln_b)


def _layer_tail(h, attn_o, ssm_o, gates, p, *, tm):
    merged = _merge(attn_o, ssm_o, gates, p["w_branch_attn"], p["w_branch_ssm"], tm=tm, tn=1024)
    h1, ids, wts = _ln1_router(merged, p["w_out"], h, p["ln1_g"], p["ln1_b"],
                               p["w_route"], p["b_route"], tm=tm)
    y = _moe_experts(h1, ids[:, :TOP_K], p["w_gate"], p["w_up"], p["w_down"], bm=MOE_BLOCK)
    return _final_norm(h1, y, wts, p["ln2_g"], p["ln2_b"], tm=tm)


def kernel(x_prompt, x_sample, cache_k, cache_v, page_table, state_conv, state_ssm, rel_bias, w_in, lambda_q1, lambda_k1, lambda_q2, lambda_k2, attn_subln, conv_w, conv_b, dt_bias, a_log, d_skip, ssm_norm_w, w_branch_attn, w_branch_ssm, w_out, ln1_g, ln1_b, w_route_group, b_route_group, w_route_expert, b_route_expert, w_gate, w_up, w_down, ln2_g, ln2_b):
    bsz, seq, _ = x_prompt.shape
    n_dec = x_sample.shape[0]
    assert x_sample.shape[1] == 1

    w_main = w_in[:, :OFF_DT].astype(BF16)
    w_dt = jnp.pad(w_in[:, OFF_DT:OFF_GATES], ((0, 0), (0, LANES - SSM_HEADS))).astype(BF16)
    w_gates = w_in[:, OFF_GATES:].astype(BF16)
    lam_rows = jnp.stack([lambda_q1, lambda_k1, lambda_q2, lambda_k2]).astype(F32)
    subln = attn_subln.reshape(1, V_DIM)
    pad_heads = lambda vec: jnp.pad(vec.astype(F32), (0, LANES - SSM_HEADS))
    conv_b2 = conv_b.reshape(1, CONV_DIM)
    dskip_row = jnp.repeat(d_skip.astype(F32), SSM_HEAD_DIM).reshape(1, D_INNER)
    normw_row = ssm_norm_w.reshape(1, D_INNER)
    n_route = MOE_GROUPS + N_EXPERTS
    tail = dict(
        w_branch_attn=w_branch_attn.astype(BF16), w_branch_ssm=w_branch_ssm.astype(BF16),
        w_out=w_out.astype(BF16), ln1_g=ln1_g.reshape(1, D_MODEL), ln1_b=ln1_b.reshape(1, D_MODEL),
        w_route=jnp.pad(jnp.concatenate([w_route_group, w_route_expert], axis=1),
                        ((0, 0), (0, LANES - n_route))),
        b_route=jnp.pad(jnp.concatenate([b_route_group, b_route_expert]),
                        (0, LANES - n_route)).reshape(1, LANES),
        w_gate=w_gate.astype(BF16), w_up=w_up.astype(BF16), w_down=w_down.astype(BF16),
        ln2_g=ln2_g.reshape(1, D_MODEL), ln2_b=ln2_b.reshape(1, D_MODEL))

    t_p = bsz * seq
    xp = x_prompt.reshape(t_p, D_MODEL)
    q, k, k_bf, v, v_bf, z, xbc, dt, gates = _in_projection(
        xp.astype(BF16), w_main, w_dt, w_gates, tm=1024, q_scale=Q_SCALE * LOG2E)
    attn_p = _prompt_attention(q.reshape(bsz, seq, ATTN_WIDTH), k_bf.reshape(bsz, seq, KV_WIDTH),
                               v_bf.reshape(bsz, seq, KV_WIDTH), rel_bias, lam_rows, subln, tq=512)
    xbc3 = xbc.reshape(bsz, seq, CONV_DIM)
    ssm_p, ssm_state_p = _ssd_prompt(
        xbc3, z.reshape(bsz, seq, D_INNER), dt.reshape(bsz, seq, LANES), conv_w, conv_b2,
        pad_heads(dt_bias).reshape(LANES, 1), pad_heads(a_log).reshape(LANES, 1), dskip_row, normw_row)
    y_prompt = _layer_tail(xp, attn_p.reshape(t_p, ATTN_WIDTH), ssm_p.reshape(t_p, D_INNER), gates,
                           tail, tm=512).reshape(bsz, seq, D_MODEL)
    k_prompt = k.reshape(bsz, seq, N_KV_HEADS, V_DIM)
    v_prompt = v.reshape(bsz, seq, N_KV_HEADS, V_DIM)
    conv_prompt = xbc3[:, seq - (CONV_WIDTH - 1):, :]
    ssm_prompt = ssm_state_p.reshape(bsz, SSM_HEADS, SSM_HEAD_DIM, D_STATE)

    xs = x_sample.reshape(n_dec, D_MODEL)
    q, k, k_bf, v, v_bf, z, xbc, dt, gates = _in_projection(
        xs.astype(BF16), w_main, w_dt, w_gates, tm=n_dec, q_scale=Q_SCALE)
    attn_s = _decode_attention(q, k, v, cache_k, cache_v, page_table, rel_bias,
                               lam_rows, subln, pages_per_step=8)
    ssm_s, conv_sample, ssm_state_s = _ssd_step(
        xbc, state_conv, z, dt, state_ssm.reshape(n_dec, D_INNER, D_STATE), conv_w, conv_b2,
        pad_heads(dt_bias).reshape(1, LANES), pad_heads(a_log).reshape(1, LANES), dskip_row, normw_row)
    y_sample = _layer_tail(xs, attn_s, ssm_s, gates, tail, tm=n_dec).reshape(n_dec, 1, D_MODEL)
    k_sample = k.reshape(n_dec, 1, N_KV_HEADS, V_DIM)
    v_sample = v.reshape(n_dec, 1, N_KV_HEADS, V_DIM)
    ssm_sample = ssm_state_s.reshape(n_dec, SSM_HEADS, SSM_HEAD_DIM, D_STATE)

    return (y_prompt, y_sample, k_prompt, v_prompt, conv_prompt, ssm_prompt,
            k_sample, v_sample, conv_sample, ssm_sample)
```

```python
import functools
import math

import jax
import jax.numpy as jnp
import numpy as np
from jax import lax
from jax.experimental import pallas as pl
from jax.experimental.pallas import tpu as pltpu

F32 = jnp.float32
BF16 = jnp.bfloat16
I32 = jnp.int32

D_MODEL = 2048
DEPTH = 1
N_HEADS = 8
N_KV_HEADS = 4
Q_PER_KV = N_HEADS // N_KV_HEADS
HEAD_DIM = D_MODEL // (2 * N_HEADS)
V_DIM = 2 * HEAD_DIM
ATTN_WIDTH = N_HEADS * V_DIM
KV_WIDTH = N_KV_HEADS * V_DIM
N_BUCKETS = 32
MAX_EXACT = N_BUCKETS // 2
MAX_DISTANCE = 128
D_INNER = 2 * D_MODEL
SSM_HEAD_DIM = 64
SSM_HEADS = D_INNER // SSM_HEAD_DIM
SSM_GROUPS = 8
SSM_HPG = SSM_HEADS // SSM_GROUPS
D_STATE = 128
CONV_WIDTH = 4
GN = SSM_GROUPS * D_STATE
CONV_DIM = D_INNER + 2 * GN
SSD_CHUNK = 128
GROUP_WIDTH = D_INNER // SSM_GROUPS
MOE_GROUPS = 4
EXPERTS_PER_GROUP = 8
N_EXPERTS = MOE_GROUPS * EXPERTS_PER_GROUP
TOP_K = 2
D_FF = D_MODEL // 2
MOE_BLOCK = 128
ALPHA = (2 * DEPTH) ** 0.25
EPS = 1e-5
LAM_INIT = 0.8 - 0.6 * math.exp(-0.3 * 0)
Q_SCALE = HEAD_DIM ** -0.5
LOG2E = math.log2(math.e)
OFF_Q, OFF_K, OFF_V, OFF_Z, OFF_XBC = 0, 2048, 3072, 4096, 8192
OFF_DT = OFF_XBC + CONV_DIM
OFF_GATES = OFF_DT + SSM_HEADS

LANES = 128
V7X_VMEM_LIMIT = 56 * 1024 * 1024


def _cparams(n_axes, vmem=V7X_VMEM_LIMIT):
    return pltpu.CompilerParams(
        dimension_semantics=("arbitrary",) * n_axes, vmem_limit_bytes=vmem)


def _sigmoid(x):
    return 1.0 / (1.0 + jnp.exp(-x))


def _silu(x):
    return x * _sigmoid(x)


def _softplus(x):
    return jnp.maximum(x, 0.0) + jnp.log(1.0 + jnp.exp(-jnp.abs(x)))


def _layer_norm(x, g, b):
    xc = x - jnp.mean(x, axis=-1, keepdims=True)
    var = jnp.mean(xc * xc, axis=-1, keepdims=True)
    return xc * lax.rsqrt(var + EPS) * g + b


def _lane_tile(x, n):
    return jnp.concatenate([x] * n, axis=1)


def _split3(v):
    hi = v.astype(BF16)
    r1 = v - hi.astype(F32)
    mid = r1.astype(BF16)
    lo = (r1 - mid.astype(F32)).astype(BF16)
    return jnp.concatenate([hi, mid, lo], axis=1)


def _proj_kernel(x_ref, w_ref, *o_refs, scale):
    acc = jnp.dot(x_ref[...], w_ref[...], preferred_element_type=F32)
    if scale != 1.0:
        acc = acc * scale
    for o_ref in o_refs:
        o_ref[...] = acc.astype(o_ref.dtype)


def _proj_heads_kernel(x_ref, w_ref, o_ref, obf_ref):
    acc = jnp.dot(x_ref[...], w_ref[...], preferred_element_type=F32)
    for g in range(N_KV_HEADS):
        o_ref[:, g, :] = acc[:, g * V_DIM:(g + 1) * V_DIM]
    obf_ref[...] = acc.astype(BF16)


def _proj_heads(x, w, col0, *, tm, name):
    t, k = x.shape
    assert col0 % KV_WIDTH == 0 and t % tm == 0
    off = col0 // KV_WIDTH
    return pl.pallas_call(
        _proj_heads_kernel,
        grid=(t // tm,),
        in_specs=[pl.BlockSpec((tm, k), lambda i: (i, 0)),
                  pl.BlockSpec((k, KV_WIDTH), lambda i: (0, off))],
        out_specs=[pl.BlockSpec((tm, N_KV_HEADS, V_DIM), lambda i: (i, 0, 0)),
                   pl.BlockSpec((tm, KV_WIDTH), lambda i: (i, 0))],
        out_shape=[jax.ShapeDtypeStruct((t, N_KV_HEADS, V_DIM), F32),
                   jax.ShapeDtypeStruct((t, KV_WIDTH), BF16)],
        compiler_params=_cparams(1),
        name=name,
    )(x, w)


def _proj(x, w, col0, ncols, out_dtypes, *, scale=1.0, tm, tn, name):
    t, k = x.shape
    off = col0 // tn
    assert col0 % tn == 0 and ncols % tn == 0 and t % tm == 0
    return pl.pallas_call(
        functools.partial(_proj_kernel, scale=scale),
        grid=(t // tm, ncols // tn),
        in_specs=[pl.BlockSpec((tm, k), lambda i, j: (i, 0)),
                  pl.BlockSpec((k, tn), lambda i, j: (0, j + off))],
        out_specs=[pl.BlockSpec((tm, tn), lambda i, j: (i, j)) for _ in out_dtypes],
        out_shape=[jax.ShapeDtypeStruct((t, ncols), dt) for dt in out_dtypes],
        compiler_params=_cparams(2),
        name=name,
    )(x, w)


def _in_projection(x_bf, w_main, w_dt, w_gates, tm, q_scale):
    tn = 1024
    (q,) = _proj(x_bf, w_main, OFF_Q, ATTN_WIDTH, [BF16], scale=q_scale, tm=tm, tn=tn, name="proj_q")
    k, k_bf = _proj_heads(x_bf, w_main, OFF_K, tm=tm, name="proj_k")
    v, v_bf = _proj_heads(x_bf, w_main, OFF_V, tm=tm, name="proj_v")
    (z,) = _proj(x_bf, w_main, OFF_Z, D_INNER, [F32], tm=tm, tn=tn, name="proj_z")
    (xbc,) = _proj(x_bf, w_main, OFF_XBC, CONV_DIM, [F32], tm=tm, tn=tn, name="proj_xbc")
    (dt,) = _proj(x_bf, w_dt, 0, LANES, [F32], tm=tm, tn=LANES, name="proj_dt")
    (gates,) = _proj(x_bf, w_gates, 0, 2 * D_MODEL, [F32], tm=tm, tn=tn, name="proj_gates")
    return q, k, k_bf, v, v_bf, z, xbc, dt, gates


def _t5_bucket(dist):
    n = jnp.maximum(dist, 0)
    nf = jnp.maximum(n, 1).astype(F32)
    large = MAX_EXACT + (jnp.log(nf / MAX_EXACT) / math.log(MAX_DISTANCE / MAX_EXACT)
                         * (N_BUCKETS - MAX_EXACT)).astype(I32)
    large = jnp.minimum(large, N_BUCKETS - 1)
    return jnp.where(n < MAX_EXACT, n, large)


def _bias_minus_far(dist, relb_ref, head):
    bucket = _t5_bucket(dist)
    far = relb_ref[N_BUCKETS - 1, head]
    out = jnp.zeros(dist.shape, F32)
    for b in range(N_BUCKETS - 1):
        out = jnp.where(bucket == b, relb_ref[b, head] - far, out)
    return out


def _lambda_value(lam_ref):
    s1 = jnp.sum(lam_ref[0:1, :] * lam_ref[1:2, :], axis=1, keepdims=True)
    s2 = jnp.sum(lam_ref[2:3, :] * lam_ref[3:4, :], axis=1, keepdims=True)
    return jnp.exp(s1) - jnp.exp(s2) + LAM_INIT


def _subln(o, subln_ref):
    ms = jnp.mean(o * o, axis=-1, keepdims=True)
    return o * lax.rsqrt(ms + EPS) * subln_ref[...] * (1.0 - LAM_INIT)


def _prompt_attn_kernel(relb_ref, lam_ref, subln_ref, q_ref, k_ref, v_ref, o_ref,
                        qs_ref, m_ref, l_ref, acc_ref, bias_ref, *, tq):
    g = pl.program_id(0)
    b = pl.program_id(1)
    qi = pl.program_id(2)
    rows = 2 * tq

    @pl.when(jnp.logical_and(b == 0, qi == 0))
    def _():
        r = lax.broadcasted_iota(I32, (tq, tq), 0)
        c = lax.broadcasted_iota(I32, (tq, tq), 1)
        for hh in range(Q_PER_KV):
            head = g * Q_PER_KV + hh
            adj = _bias_minus_far(tq + r - c, relb_ref, head) * LOG2E
            diag = _bias_minus_far(r - c, relb_ref, head) * LOG2E
            diag = jnp.where(r >= c, diag, -jnp.inf)
            bias_ref[0, hh * tq:(hh + 1) * tq, :] = adj
            bias_ref[1, hh * tq:(hh + 1) * tq, :] = diag

    for hh in range(Q_PER_KV):
        for mm in range(2):
            c0 = hh * V_DIM + mm * HEAD_DIM
            qs_ref[mm, hh * tq:(hh + 1) * tq, :] = q_ref[0, :, c0:c0 + HEAD_DIM]

    m_ref[...] = jnp.full(m_ref.shape, -jnp.inf, F32)
    l_ref[...] = jnp.zeros(l_ref.shape, F32)
    acc_ref[...] = jnp.zeros(acc_ref.shape, F32)

    chains = [(mm, hh) for mm in range(2) for hh in range(Q_PER_KV)]

    def key_tile_start(ki):
        return pl.multiple_of(ki * tq, tq)

    def chain_update(mm, hh, ki, bias_kind):
        rs = slice(hh * tq, (hh + 1) * tq)
        kt = k_ref[0, pl.ds(key_tile_start(ki), tq), mm * HEAD_DIM:(mm + 1) * HEAD_DIM]
        vt = v_ref[0, pl.ds(key_tile_start(ki), tq), :]
        s = lax.dot_general(qs_ref[mm, rs, :], kt, (((1,), (1,)), ((), ())),
                            preferred_element_type=F32)
        if bias_kind is not None:
            s = s + bias_ref[bias_kind, rs, :]
        m_prev = m_ref[mm, rs, :]
        m_new = jnp.maximum(m_prev, jnp.max(s, axis=1, keepdims=True))
        p = jnp.exp2(s - _lane_tile(m_new, tq // LANES))
        alpha = jnp.exp2(m_prev - m_new)
        l_ref[mm, rs, :] = alpha * l_ref[mm, rs, :] + jnp.sum(p, axis=1, keepdims=True)
        acc_ref[mm, rs, :] = (acc_ref[mm, rs, :] * _lane_tile(alpha, V_DIM // LANES)
                              + jnp.dot(p.astype(BF16), vt, preferred_element_type=F32))
        m_ref[mm, rs, :] = m_new

    def tile_update(ki, bias_kind):
        for mm, hh in chains:
            chain_update(mm, hh, ki, bias_kind)

    def far_body(ki, carry):
        tile_update(ki, None)
        return carry
    lax.fori_loop(0, qi - 1, far_body, 0)

    @pl.when(qi >= 1)
    def _():
        tile_update(qi - 1, 0)

    tile_update(qi, 1)

    lam = _lambda_value(lam_ref)
    o1 = acc_ref[0] / _lane_tile(l_ref[0], V_DIM // LANES)
    o2 = acc_ref[1] / _lane_tile(l_ref[1], V_DIM // LANES)
    o = _subln(o1 - lam * o2, subln_ref)
    for hh in range(Q_PER_KV):
        o_ref[0, :, hh * V_DIM:(hh + 1) * V_DIM] = o[hh * tq:(hh + 1) * tq].astype(o_ref.dtype)


def _prompt_attention(q, k_bf, v_bf, rel_bias, lam_rows, subln, *, tq):
    bsz, length, _ = q.shape
    assert length % tq == 0 and tq % LANES == 0
    gw = Q_PER_KV * V_DIM
    rows = 2 * tq
    return pl.pallas_call(
        functools.partial(_prompt_attn_kernel, tq=tq),
        grid=(N_KV_HEADS, bsz, length // tq),
        in_specs=[
            pl.BlockSpec(memory_space=pltpu.SMEM),
            pl.BlockSpec((4, HEAD_DIM), lambda g, b, i: (0, 0)),
            pl.BlockSpec((1, V_DIM), lambda g, b, i: (0, 0)),
            pl.BlockSpec((1, tq, gw), lambda g, b, i: (b, i, g)),
            pl.BlockSpec((1, length, V_DIM), lambda g, b, i: (b, 0, g)),
            pl.BlockSpec((1, length, V_DIM), lambda g, b, i: (b, 0, g)),
        ],
        out_specs=pl.BlockSpec((1, tq, gw), lambda g, b, i: (b, i, g)),
        out_shape=jax.ShapeDtypeStruct((bsz, length, ATTN_WIDTH), BF16),
        scratch_shapes=[
            pltpu.VMEM((2, rows, HEAD_DIM), BF16),
            pltpu.VMEM((2, rows, LANES), F32),
            pltpu.VMEM((2, rows, LANES), F32),
            pltpu.VMEM((2, rows, V_DIM), F32),
            pltpu.VMEM((2, rows, tq), F32),
        ],
        compiler_params=_cparams(3),
        name="prompt_attention",
    )(rel_bias, lam_rows, subln, q, k_bf, v_bf)


def _decode_attn_kernel(pt_ref, relb_ref, lam_ref, subln_ref, q_ref, kn_ref, vn_ref, *rest,
                        pages_per_step, page, past_len):
    del pt_ref
    kp = rest[:pages_per_step]
    vp = rest[pages_per_step:2 * pages_per_step]
    o_ref, m_ref, l_ref, acc_ref = rest[2 * pages_per_step:]
    j = pl.program_id(1)
    n_steps = pl.num_programs(1)
    rpt = 2 * N_KV_HEADS
    prow = page * rpt
    nq = 2 * N_HEADS

    @pl.when(j == 0)
    def _():
        m_ref[...] = jnp.full(m_ref.shape, -jnp.inf, F32)
        l_ref[...] = jnp.zeros(l_ref.shape, F32)
        acc_ref[...] = jnp.zeros(acc_ref.shape, F32)

    q_all = q_ref[0]

    def row_head(shape):
        r = lax.broadcasted_iota(I32, shape, 0)
        return (r % N_KV_HEADS) * Q_PER_KV + (r // N_KV_HEADS) % Q_PER_KV

    def row_bias(dist):
        out = jnp.zeros(dist.shape, F32)
        head = row_head(dist.shape)
        for h in range(N_HEADS):
            out = jnp.where(head == h, _bias_minus_far(dist, relb_ref, h), out)
        return out

    def online(s, pv_fn):
        m_prev = m_ref[...]
        m_new = jnp.maximum(m_prev, jnp.max(s, axis=1, keepdims=True))
        p = jnp.exp(s - m_new[:, :1])
        alpha = jnp.exp(m_prev - m_new)
        l_ref[...] = alpha * l_ref[...] + jnp.sum(p, axis=1, keepdims=True)
        acc_ref[...] = acc_ref[...] * jnp.concatenate([alpha, alpha], axis=0) + pv_fn(p)
        m_ref[...] = m_new

    s_parts = [lax.dot_general(q_all, kp[t][0].astype(BF16), (((1,), (1,)), ((), ())),
                               preferred_element_type=F32) for t in range(pages_per_step)]

    def add_near_bias(s_last):
        c = lax.broadcasted_iota(I32, (nq, prow), 1)
        pos = (j * pages_per_step + pages_per_step - 1) * page + c // rpt
        return s_last + row_bias(past_len - pos)

    s_parts[-1] = lax.cond(j == n_steps - 1, add_near_bias, lambda s: s, s_parts[-1])
    s = jnp.concatenate(s_parts, axis=1)
    col = lax.broadcasted_iota(I32, s.shape, 1)
    row = lax.broadcasted_iota(I32, s.shape, 0)
    own = col % rpt == (row // (nq // 2)) * N_KV_HEADS + row % N_KV_HEADS
    s = jnp.where(own, s, -jnp.inf)

    def pv(p):
        width = p.shape[1]
        up = pltpu.roll(p, N_KV_HEADS, axis=1)
        down = pltpu.roll(p, width - N_KV_HEADS, axis=1)
        top = row[:, :1] < nq // 2
        p_half0 = jnp.where(top, p, down)
        p_half1 = jnp.where(top, up, p)
        lhs = jnp.concatenate([p_half0, p_half1], axis=0).astype(BF16)
        out = jnp.zeros((2 * nq, LANES), F32)
        for t in range(pages_per_step):
            out = out + jnp.dot(lhs[:, t * prow:(t + 1) * prow], vp[t][0].astype(BF16),
                                preferred_element_type=F32)
        return out

    online(s, pv)

    @pl.when(j == n_steps - 1)
    def _():
        def per_row(ref, part):
            blk = ref[0, :, part * LANES:(part + 1) * LANES].astype(BF16).astype(F32)
            return jnp.concatenate([blk] * Q_PER_KV, axis=0)

        kn = jnp.concatenate([per_row(kn_ref, 0), per_row(kn_ref, 1)], axis=0)
        s_new = jnp.sum(q_all.astype(F32) * kn, axis=1, keepdims=True)
        s_new = s_new + row_bias(jnp.zeros((nq, 1), I32))
        vn = jnp.concatenate([per_row(vn_ref, 0)] * 2 + [per_row(vn_ref, 1)] * 2, axis=0)
        online(s_new, lambda p: jnp.concatenate([p, p], axis=0).astype(BF16).astype(F32) * vn)

        lam = _lambda_value(lam_ref)
        l_all = l_ref[...]
        o_lo = acc_ref[0:nq] / l_all
        o_hi = acc_ref[nq:2 * nq] / l_all
        o_all = jnp.concatenate([o_lo, o_hi], axis=1)
        o = _subln(o_all[0:nq // 2] - lam * o_all[nq // 2:nq], subln_ref)
        for hh in range(Q_PER_KV):
            for g in range(N_KV_HEADS):
                h = g * Q_PER_KV + hh
                r = hh * N_KV_HEADS + g
                o_ref[0, :, h * V_DIM:(h + 1) * V_DIM] = o[r:r + 1].astype(o_ref.dtype)


def _decode_attention(q, k_new, v_new, cache_k, cache_v, page_table, rel_bias, lam_rows, subln,
                      *, pages_per_step):
    n_seq = q.shape[0]
    page = cache_k.shape[1]
    n_pages = page_table.shape[1]
    assert n_pages % pages_per_step == 0
    past_len = n_pages * page
    q3 = (q.reshape(n_seq, N_KV_HEADS, Q_PER_KV, 2, HEAD_DIM).transpose(0, 3, 2, 1, 4)
          .reshape(n_seq, N_HEADS * 2, HEAD_DIM))
    prow = page * 2 * N_KV_HEADS

    def stored_rows(cache):
        return (cache.reshape(-1, page, N_KV_HEADS, 2, LANES).transpose(0, 1, 3, 2, 4)
                .reshape(-1, prow, LANES))

    def page_spec(t):
        return pl.BlockSpec((1, prow, LANES),
                            lambda i, j, pt, t=t: (pt[i, j * pages_per_step + t], 0, 0))

    grid_spec = pltpu.PrefetchScalarGridSpec(
        num_scalar_prefetch=1,
        grid=(n_seq, n_pages // pages_per_step),
        in_specs=[
            pl.BlockSpec(memory_space=pltpu.SMEM),
            pl.BlockSpec((4, HEAD_DIM), lambda i, j, pt: (0, 0)),
            pl.BlockSpec((1, V_DIM), lambda i, j, pt: (0, 0)),
            pl.BlockSpec((1, N_HEADS * 2, HEAD_DIM), lambda i, j, pt: (i, 0, 0)),
            pl.BlockSpec((1, N_KV_HEADS, V_DIM), lambda i, j, pt: (i, 0, 0)),
            pl.BlockSpec((1, N_KV_HEADS, V_DIM), lambda i, j, pt: (i, 0, 0)),
        ] + [page_spec(t) for t in range(pages_per_step)] * 2,
        out_specs=pl.BlockSpec((1, 1, ATTN_WIDTH), lambda i, j, pt: (i, 0, 0)),
        scratch_shapes=[
            pltpu.VMEM((2 * N_HEADS, LANES), F32),
            pltpu.VMEM((2 * N_HEADS, LANES), F32),
            pltpu.VMEM((4 * N_HEADS, LANES), F32),
        ],
    )
    out = pl.pallas_call(
        functools.partial(_decode_attn_kernel, pages_per_step=pages_per_step, page=page,
                          past_len=past_len),
        grid_spec=grid_spec,
        out_shape=jax.ShapeDtypeStruct((n_seq, 1, ATTN_WIDTH), BF16),
        compiler_params=_cparams(2),
        name="decode_attention",
    )(page_table, rel_bias, lam_rows, subln, q3, k_new, v_new,
      *([stored_rows(cache_k)] * pages_per_step), *([stored_rows(cache_v)] * pages_per_step))
    return out.reshape(n_seq, ATTN_WIDTH)


def _gate_and_norm(y, z, normw):
    y = y * _silu(z)
    parts = []
    for g in range(SSM_GROUPS):
        yg = y[:, g * GROUP_WIDTH:(g + 1) * GROUP_WIDTH]
        ms = jnp.mean(yg * yg, axis=-1, keepdims=True)
        parts.append(yg * lax.rsqrt(ms + EPS))
    return jnp.concatenate(parts, axis=1) * normw


def _ssd_prompt_kernel(xbc_ref, z_ref, dt_ref, convw_ref, convb_ref, dtb_ref, alog_ref,
                       dskip_ref, normw_ref, o_ref, state_ref,
                       ext_ref, act_ref, st_ref, y_ref, e3_ref, *, chunk):
    c = pl.program_id(1)
    n_chunks = pl.num_programs(1)
    halo = CONV_WIDTH - 1
    base = 8
    pair_w = 2 * SSM_HEAD_DIM

    @pl.when(c == 0)
    def _():
        ext_ref[0:base, :] = jnp.zeros((base, CONV_DIM), F32)
        st_ref[...] = jnp.zeros(st_ref.shape, F32)
        r = lax.broadcasted_iota(I32, e3_ref.shape, 0) % LANES
        col = lax.broadcasted_iota(I32, e3_ref.shape, 1)
        e3_ref[...] = jnp.where(col // SSM_HEAD_DIM == r, 1.0, 0.0).astype(BF16)

    ext_ref[base:base + chunk, :] = xbc_ref[0]
    col_tile = 512
    for j in range(CONV_DIM // col_tile):
        cs = slice(j * col_tile, (j + 1) * col_tile)
        conv = convb_ref[:, cs] + jnp.zeros((chunk, col_tile), F32)
        for w in range(CONV_WIDTH):
            lo = base - halo + w
            conv = conv + ext_ref[lo:lo + chunk, cs] * convw_ref[w:w + 1, cs]
        act_ref[:, cs] = _silu(conv)
    ext_ref[base - halo:base, :] = ext_ref[base + chunk - halo:base + chunk, :]

    dt_t = _softplus(jnp.transpose(dt_ref[0]) + dtb_ref[...])
    a_t = dt_t * (-jnp.exp(alog_ref[...]))
    tt = lax.broadcasted_iota(I32, (chunk, chunk), 0)
    ss = lax.broadcasted_iota(I32, (chunk, chunk), 1)
    upper = jnp.where(tt <= ss, 1.0, 0.0).astype(BF16)
    upper3 = jnp.concatenate([upper, upper, upper], axis=0)
    acs_t = jnp.dot(_split3(a_t), upper3, preferred_element_type=F32)
    acs = jnp.transpose(acs_t)
    w_t = dt_t * jnp.exp(acs_t[:, chunk - 1:chunk] - acs_t)
    eexp = jnp.dot(_split3(jnp.exp(acs)), e3_ref[...],
                   preferred_element_type=F32)
    causal = tt >= ss

    for g in range(SSM_GROUPS):
        bg = act_ref[:, D_INNER + g * D_STATE:D_INNER + (g + 1) * D_STATE]
        cg = act_ref[:, D_INNER + GN + g * D_STATE:D_INNER + GN + (g + 1) * D_STATE]
        bg_bf = bg.astype(BF16)
        cg_bf = cg.astype(BF16)
        cb = lax.dot_general(cg_bf, bg_bf, (((1,), (1,)), ((), ())),
                             preferred_element_type=F32)
        bg_t = jnp.transpose(bg)
        gsl = slice(g * GROUP_WIDTH, (g + 1) * GROUP_WIDTH)
        y_off = jnp.dot(cg_bf, st_ref[:, gsl].astype(BF16), preferred_element_type=F32)
        for pr in range(SSM_HPG // 2):
            h0 = g * SSM_HPG + 2 * pr
            psl = slice(h0 * SSM_HEAD_DIM, h0 * SSM_HEAD_DIM + pair_w)
            lhs_d, lhs_s = [], []
            for h in (h0, h0 + 1):
                seg = acs[:, h:h + 1] - acs_t[h:h + 1, :]
                dec = jnp.exp(jnp.where(causal, seg, -jnp.inf))
                lhs_d.append((cb * dec * dt_t[h:h + 1, :]).astype(BF16))
                lhs_s.append((bg_t * w_t[h:h + 1, :]).astype(BF16))
            lhs = jnp.concatenate([jnp.concatenate(lhs_d, axis=1),
                                   jnp.concatenate(lhs_s, axis=1)], axis=0)
            xs = act_ref[:, psl]
            lane = lax.broadcasted_iota(I32, xs.shape, 1)
            rhs = jnp.concatenate([jnp.where(lane < SSM_HEAD_DIM, xs, 0.0),
                                   jnp.where(lane >= SSM_HEAD_DIM, xs, 0.0)], axis=0).astype(BF16)
            res = jnp.dot(lhs, rhs, preferred_element_type=F32)
            y_ref[:, psl] = (res[:chunk] + y_off[:, 2 * pr * SSM_HEAD_DIM:2 * pr * SSM_HEAD_DIM + pair_w]
                             * eexp[:, psl] + dskip_ref[:, psl] * xs)
            st_ref[:, psl] = st_ref[:, psl] * eexp[chunk - 1:chunk, psl] + res[chunk:]

    o_ref[0] = _gate_and_norm(y_ref[...], z_ref[0], normw_ref[...]).astype(o_ref.dtype)

    @pl.when(c == n_chunks - 1)
    def _():
        for j in range(D_INNER // LANES):
            state_ref[0, j * LANES:(j + 1) * LANES, :] = jnp.transpose(
                st_ref[:, j * LANES:(j + 1) * LANES])


def _ssd_prompt(xbc, z, dt, conv_w, conv_b, dtb_col, alog_col, dskip_row, norm_w):
    bsz, length, _ = xbc.shape
    chunk = SSD_CHUNK
    assert length % chunk == 0

    def const(shape):
        return pl.BlockSpec(shape, lambda b, c: (0,) * len(shape))

    return pl.pallas_call(
        functools.partial(_ssd_prompt_kernel, chunk=chunk),
        grid=(bsz, length // chunk),
        in_specs=[
            pl.BlockSpec((1, chunk, CONV_DIM), lambda b, c: (b, c, 0)),
            pl.BlockSpec((1, chunk, D_INNER), lambda b, c: (b, c, 0)),
            pl.BlockSpec((1, chunk, LANES), lambda b, c: (b, c, 0)),
            const((CONV_WIDTH, CONV_DIM)), const((1, CONV_DIM)),
            const((LANES, 1)), const((LANES, 1)),
            const((1, D_INNER)), const((1, D_INNER)),
        ],
        out_specs=[
            pl.BlockSpec((1, chunk, D_INNER), lambda b, c: (b, c, 0)),
            pl.BlockSpec((1, D_INNER, D_STATE), lambda b, c: (b, 0, 0)),
        ],
        out_shape=[
            jax.ShapeDtypeStruct((bsz, length, D_INNER), BF16),
            jax.ShapeDtypeStruct((bsz, D_INNER, D_STATE), F32),
        ],
        scratch_shapes=[
            pltpu.VMEM((8 + chunk, CONV_DIM), F32),
            pltpu.VMEM((chunk, CONV_DIM), F32),
            pltpu.VMEM((D_STATE, D_INNER), F32),
            pltpu.VMEM((chunk, D_INNER), F32),
            pltpu.VMEM((3 * LANES, D_INNER), BF16),
        ],
        compiler_params=_cparams(2),
        name="ssd_prompt",
    )(xbc, z, dt, conv_w, conv_b, dtb_col, alog_col, dskip_row, norm_w)


def _row_to_col(row):
    r = lax.broadcasted_iota(I32, (LANES, LANES), 0)
    c = lax.broadcasted_iota(I32, (LANES, LANES), 1)
    return jnp.sum(jnp.where(r == c, jnp.broadcast_to(row, (LANES, LANES)), 0.0),
                   axis=1, keepdims=True)


def _ssd_step_kernel(xbc_ref, sconv_ref, z_ref, dt_ref, state_ref, convw_ref, convb_ref,
                     dtb_ref, alog_ref, dskip_ref, normw_ref,
                     o_ref, conv_out_ref, state_out_ref, e3_ref):
    halo = CONV_WIDTH - 1

    @pl.when(pl.program_id(0) == 0)
    def _():
        r = lax.broadcasted_iota(I32, e3_ref.shape, 0) % LANES
        col = lax.broadcasted_iota(I32, e3_ref.shape, 1)
        e3_ref[...] = jnp.where(col // SSM_HEAD_DIM == r, 1.0, 0.0).astype(BF16)

    x_new = xbc_ref[0]
    conv = convb_ref[...] + x_new * convw_ref[halo:halo + 1, :]
    for w in range(halo):
        conv = conv + sconv_ref[0, w:w + 1, :] * convw_ref[w:w + 1, :]
    act = _silu(conv)
    conv_out_ref[0, 0:halo - 1, :] = sconv_ref[0, 1:halo, :]
    conv_out_ref[0, halo - 1:halo, :] = x_new

    dt = _softplus(dt_ref[0] + dtb_ref[...])
    decay = jnp.exp(dt * (-jnp.exp(alog_ref[...])))
    both = jnp.concatenate([decay, dt, jnp.zeros((6, LANES), F32)], axis=0)
    expanded = jnp.dot(_split3(both), e3_ref[...], preferred_element_type=F32)
    decay_x = expanded[0:1]
    xs = act[:, :D_INNER]
    xdt = xs * expanded[1:2]

    y_parts = []
    for g in range(SSM_GROUPS):
        b_row = act[:, D_INNER + g * D_STATE:D_INNER + (g + 1) * D_STATE]
        c_row = act[:, D_INNER + GN + g * D_STATE:D_INNER + GN + (g + 1) * D_STATE]
        for k in range(GROUP_WIDTH // LANES):
            r0 = g * GROUP_WIDTH + k * LANES
            new = (state_ref[0, r0:r0 + LANES, :] * _row_to_col(decay_x[:, r0:r0 + LANES])
                   + _row_to_col(xdt[:, r0:r0 + LANES]) * b_row)
            state_out_ref[0, r0:r0 + LANES, :] = new
        h_new = state_out_ref[0, g * GROUP_WIDTH:(g + 1) * GROUP_WIDTH, :].astype(BF16)
        c8 = jnp.broadcast_to(c_row, (8, D_STATE)).astype(BF16)
        y_parts.append(lax.dot_general(c8, h_new, (((1,), (1,)), ((), ())),
                                       preferred_element_type=F32)[0:1])
    y = jnp.concatenate(y_parts, axis=1) + dskip_ref[...] * xs
    o_ref[0] = _gate_and_norm(y, z_ref[0], normw_ref[...]).astype(o_ref.dtype)


def _ssd_step(xbc, state_conv, z, dt, state_ssm, conv_w, conv_b, dtb_row, alog_row, dskip_row, norm_w):
    n_seq = xbc.shape[0]
    halo = CONV_WIDTH - 1

    def const(shape):
        return pl.BlockSpec(shape, lambda i: (0,) * len(shape))

    y, conv_out, state_out = pl.pallas_call(
        _ssd_step_kernel,
        grid=(n_seq,),
        in_specs=[
            pl.BlockSpec((1, 1, CONV_DIM), lambda i: (i, 0, 0)),
            pl.BlockSpec((1, halo, CONV_DIM), lambda i: (i, 0, 0)),
            pl.BlockSpec((1, 1, D_INNER), lambda i: (i, 0, 0)),
            pl.BlockSpec((1, 1, LANES), lambda i: (i, 0, 0)),
            pl.BlockSpec((1, D_INNER, D_STATE), lambda i: (i, 0, 0)),
            const((CONV_WIDTH, CONV_DIM)), const((1, CONV_DIM)),
            const((1, LANES)), const((1, LANES)),
            const((1, D_INNER)), const((1, D_INNER)),
        ],
        out_specs=[
            pl.BlockSpec((1, 1, D_INNER), lambda i: (i, 0, 0)),
            pl.BlockSpec((1, halo, CONV_DIM), lambda i: (i, 0, 0)),
            pl.BlockSpec((1, D_INNER, D_STATE), lambda i: (i, 0, 0)),
        ],
        out_shape=[
            jax.ShapeDtypeStruct((n_seq, 1, D_INNER), BF16),
            jax.ShapeDtypeStruct((n_seq, halo, CONV_DIM), F32),
            jax.ShapeDtypeStruct((n_seq, D_INNER, D_STATE), F32),
        ],
        scratch_shapes=[pltpu.VMEM((3 * LANES, D_INNER), BF16)],
        compiler_params=_cparams(1),
        name="ssd_step",
    )(xbc.reshape(n_seq, 1, CONV_DIM), state_conv, z.reshape(n_seq, 1, D_INNER),
      dt.reshape(n_seq, 1, LANES), state_ssm, conv_w, conv_b, dtb_row, alog_row, dskip_row, norm_w)
    return y.reshape(n_seq, D_INNER), conv_out, state_out


def _merge_kernel(attn_ref, ssm_ref, wa_ref, ws_ref, ga_ref, gb_ref, o_ref):
    a = jnp.dot(attn_ref[...], wa_ref[...], preferred_element_type=F32)
    s = jnp.dot(ssm_ref[...], ws_ref[...], preferred_element_type=F32)
    o_ref[...] = (_sigmoid(ga_ref[...]) * a + _sigmoid(gb_ref[...]) * s).astype(o_ref.dtype)


def _merge(attn_o, ssm_o, gates, w_attn, w_ssm, *, tm, tn):
    t = attn_o.shape[0]
    nb = D_MODEL // tn
    return pl.pallas_call(
        _merge_kernel,
        grid=(t // tm, nb),
        in_specs=[
            pl.BlockSpec((tm, ATTN_WIDTH), lambda i, j: (i, 0)),
            pl.BlockSpec((tm, D_INNER), lambda i, j: (i, 0)),
            pl.BlockSpec((ATTN_WIDTH, tn), lambda i, j: (0, j)),
            pl.BlockSpec((D_INNER, tn), lambda i, j: (0, j)),
            pl.BlockSpec((tm, tn), lambda i, j: (i, j)),
            pl.BlockSpec((tm, tn), lambda i, j: (i, j + nb)),
        ],
        out_specs=pl.BlockSpec((tm, tn), lambda i, j: (i, j)),
        out_shape=jax.ShapeDtypeStruct((t, D_MODEL), BF16),
        compiler_params=_cparams(2),
        name="branch_merge",
    )(attn_o, ssm_o, w_attn, w_ssm, gates, gates)


def _ln1_router_kernel(m_ref, wout_ref, h_ref, g_ref, b_ref, wr_ref, br_ref,
                       h1_ref, ids_ref, wts_ref):
    x = ALPHA * h_ref[...] + jnp.dot(m_ref[...], wout_ref[...], preferred_element_type=F32)
    h1 = _layer_norm(x, g_ref[...], b_ref[...])
    h1_ref[...] = h1

    h_hi = h1.astype(BF16)
    h_lo = (h1 - h_hi.astype(F32)).astype(BF16)
    by_hi = jnp.dot(h_hi, wr_ref[...], preferred_element_type=F32)
    logits = (by_hi[:, :LANES] + by_hi[:, LANES:]
              + jnp.dot(h_lo, wr_ref[:, :LANES], preferred_element_type=F32) + br_ref[...])
    lane = lax.broadcasted_iota(I32, logits.shape, 1)
    big = jnp.int32(LANES)
    neg = -jnp.inf
    gl = jnp.where(lane < MOE_GROUPS, logits, neg)
    gmax = jnp.max(gl, axis=1, keepdims=True)
    g_idx = jnp.min(jnp.where(gl == gmax, lane, big), axis=1, keepdims=True)
    g_w = 1.0 / jnp.sum(jnp.exp(gl - gmax), axis=1, keepdims=True)
    e_lo = MOE_GROUPS + g_idx * EXPERTS_PER_GROUP
    el = jnp.where(jnp.logical_and(lane >= e_lo, lane < e_lo + EXPERTS_PER_GROUP), logits, neg)
    e1 = jnp.max(el, axis=1, keepdims=True)
    i1 = jnp.min(jnp.where(el == e1, lane, big), axis=1, keepdims=True)
    el2 = jnp.where(lane == i1, neg, el)
    e2 = jnp.max(el2, axis=1, keepdims=True)
    i2 = jnp.min(jnp.where(el2 == e2, lane, big), axis=1, keepdims=True)
    t2 = jnp.exp(e2 - e1)
    den = 1.0 + t2
    w1 = g_w * (1.0 / den)
    w2 = g_w * (t2 / den)
    ids_ref[...] = jnp.where(lane == 0, i1 - MOE_GROUPS, jnp.where(lane == 1, i2 - MOE_GROUPS, 0))
    wts_ref[...] = jnp.where(lane == 0, w1, jnp.where(lane == 1, w2, 0.0))


def _ln1_router(merged, w_out, h, ln_g, ln_b, w_route, b_route, *, tm):
    t = merged.shape[0]

    def const(shape):
        return pl.BlockSpec(shape, lambda i: (0,) * len(shape))

    def rows(width):
        return pl.BlockSpec((tm, width), lambda i: (i, 0))

    return pl.pallas_call(
        _ln1_router_kernel,
        grid=(t // tm,),
        in_specs=[rows(D_MODEL), const((D_MODEL, D_MODEL)), rows(D_MODEL),
                  const((1, D_MODEL)), const((1, D_MODEL)),
                  const((D_MODEL, 2 * LANES)), const((1, LANES))],
        out_specs=[rows(D_MODEL), rows(LANES), rows(LANES)],
        out_shape=[jax.ShapeDtypeStruct((t, D_MODEL), F32),
                   jax.ShapeDtypeStruct((t, LANES), I32),
                   jax.ShapeDtypeStruct((t, LANES), F32)],
        compiler_params=_cparams(1),
        name="ln1_router",
    )(merged, w_out, h, ln_g, ln_b, w_route, b_route)


MOE_X_SLOTS = 3
MOE_Y_SLOTS = 2
MOE_DRAIN_STEPS = 2


def _moe_kernel(be_ref, nused_ref, g0_ref, g1_ref, g2_ref, sidx_ref, x_hbm, wg_ref, wu_ref, wd_ref,
                y_hbm, xbuf, ybuf, zrow, gsem, ssem, *, bm):
    del be_ref
    i = pl.program_id(0)
    n_used = nused_ref[0]
    xs = i % MOE_X_SLOTS
    ys = i % MOE_Y_SLOTS
    slab = y_hbm.shape[0] // TOP_K

    def gather_start(idx_ref, slot):
        for j in range(bm):
            pltpu.make_async_copy(x_hbm.at[pl.ds(idx_ref[0, 0, j], 1), :],
                                  xbuf.at[slot, pl.ds(j, 1), :], gsem.at[slot]).start()

    def gather_wait(slot):
        for j in range(bm):
            pltpu.make_async_copy(x_hbm.at[pl.ds(0, 1), :],
                                  xbuf.at[slot, pl.ds(j, 1), :], gsem.at[slot]).wait()

    def scatter_start(slot):
        for j in range(bm):
            pltpu.make_async_copy(ybuf.at[slot, pl.ds(j, 1), :],
                                  y_hbm.at[pl.ds(sidx_ref[0, 0, j], 1), :], ssem.at[slot]).start()

    def result_slot_wait(slot):
        for j in range(bm):
            pltpu.make_async_copy(ybuf.at[slot, pl.ds(j, 1), :],
                                  y_hbm.at[pl.ds(0, 1), :], ssem.at[slot]).wait()

    @pl.when(i == 0)
    def _():
        zrow[...] = jnp.zeros(zrow.shape, F32)
        for s in range(MOE_Y_SLOTS):
            for j in range(bm):
                pltpu.make_async_copy(zrow, y_hbm.at[pl.ds((s + 1) * slab - bm + j, 1), :],
                                      ssem.at[s]).start()
        gather_start(g0_ref, 0)
        gather_start(g1_ref, 1)

    @pl.when(i < n_used)
    def _():
        result_slot_wait(ys)
        gather_wait(xs)
        x = xbuf[xs].astype(BF16)
        gate = jnp.dot(x, wg_ref[0], preferred_element_type=F32)
        up = jnp.dot(x, wu_ref[0], preferred_element_type=F32)
        gather_start(g2_ref, (i + 2) % MOE_X_SLOTS)
        hmid = (_silu(gate) * up).astype(BF16)
        ybuf[ys] = jnp.dot(hmid, wd_ref[0], preferred_element_type=F32)
        scatter_start(ys)

    @pl.when(jnp.logical_and(i >= n_used, i < n_used + MOE_DRAIN_STEPS))
    def _():
        result_slot_wait(ys)
        gather_wait(xs)


def _moe_experts(x, ids, w_gate, w_up, w_down, *, bm):
    t = x.shape[0]
    n_assign = t * TOP_K
    n_blocks = -(-(n_assign + N_EXPERTS * (bm - 1)) // bm) + MOE_DRAIN_STEPS
    rows = n_blocks * bm
    flat = ids.reshape(-1)
    order = jnp.argsort(flat, stable=True).astype(I32)
    sorted_e = flat[order]
    counts = jnp.bincount(flat, length=N_EXPERTS).astype(I32)
    starts = jnp.cumsum(counts) - counts
    padded = (counts + bm - 1) // bm * bm
    pad_ends = jnp.cumsum(padded)
    pad_starts = pad_ends - padded
    dest_sorted = pad_starts[sorted_e] + jnp.arange(n_assign, dtype=I32) - starts[sorted_e]
    row_assign = jnp.full((rows,), n_assign, I32).at[dest_sorted].set(order)
    n_used = (pad_ends[-1] // bm).astype(I32)
    blk = jnp.arange(n_blocks, dtype=I32)
    block_expert = jnp.minimum(jnp.searchsorted(pad_ends, blk * bm, side='right'), N_EXPERTS - 1).astype(I32)
    last_used = block_expert[jnp.maximum(n_used - 1, 0)]
    block_expert = jnp.where(blk < n_used, block_expert, last_used)
    valid = row_assign < n_assign
    pos = jnp.arange(rows, dtype=I32)
    gather_idx = jnp.where(valid, row_assign // TOP_K, 0).reshape(n_blocks, 1, bm)
    assert TOP_K == 2
    slab = t + bm
    dump = ((pos // bm) % MOE_Y_SLOTS) * slab + t + pos % bm
    dest = (row_assign % TOP_K) * slab + row_assign // TOP_K
    scatter_idx = jnp.where(valid, dest, dump).reshape(n_blocks, 1, bm)

    def idx_spec(shift):
        return pl.BlockSpec((1, 1, bm), lambda i, be, nu: (jnp.minimum(i + shift, n_blocks - 1), 0, 0),
                            memory_space=pltpu.SMEM)

    def w_spec(shape):
        return pl.BlockSpec((1,) + shape, lambda i, be, nu: (be[i], 0, 0))

    grid_spec = pltpu.PrefetchScalarGridSpec(
        num_scalar_prefetch=2,
        grid=(n_blocks,),
        in_specs=[idx_spec(0), idx_spec(1), idx_spec(2), idx_spec(0),
                  pl.BlockSpec(memory_space=pl.ANY),
                  w_spec((D_MODEL, D_FF)), w_spec((D_MODEL, D_FF)), w_spec((D_FF, D_MODEL))],
        out_specs=pl.BlockSpec(memory_space=pl.ANY),
        scratch_shapes=[pltpu.VMEM((MOE_X_SLOTS, bm, D_MODEL), F32),
                        pltpu.VMEM((MOE_Y_SLOTS, bm, D_MODEL), F32),
                        pltpu.VMEM((1, D_MODEL), F32),
                        pltpu.SemaphoreType.DMA((MOE_X_SLOTS,)),
                        pltpu.SemaphoreType.DMA((MOE_Y_SLOTS,))],
    )
    y = pl.pallas_call(
        functools.partial(_moe_kernel, bm=bm),
        grid_spec=grid_spec,
        out_shape=jax.ShapeDtypeStruct((TOP_K * slab, D_MODEL), F32),
        compiler_params=_cparams(1),
        name="moe_experts",
    )(block_expert, n_used.reshape(1), gather_idx, gather_idx, gather_idx, scatter_idx, x,
      w_gate, w_up, w_down)
    return y.reshape(TOP_K, slab, D_MODEL)


def _final_kernel(h1_ref, *rest):
    y_refs = rest[:TOP_K]
    wts_ref, g_ref, b_ref, o_ref = rest[TOP_K:]
    ffn = jnp.zeros(h1_ref.shape, F32)
    for k in range(TOP_K):
        ffn = ffn + wts_ref[:, k:k + 1] * y_refs[k][0]
    o_ref[...] = _layer_norm(ALPHA * h1_ref[...] + ffn, g_ref[...], b_ref[...])


def _final_norm(h1, y, wts, ln_g, ln_b, *, tm):
    t = h1.shape[0]

    def const(shape):
        return pl.BlockSpec(shape, lambda i: (0,) * len(shape))

    return pl.pallas_call(
        _final_kernel,
        grid=(t // tm,),
        in_specs=[pl.BlockSpec((tm, D_MODEL), lambda i: (i, 0))]
        + [pl.BlockSpec((1, tm, D_MODEL), lambda i, k=k: (k, i, 0)) for k in range(TOP_K)]
        + [pl.BlockSpec((tm, LANES), lambda i: (i, 0)),
           const((1, D_MODEL)), const((1, D_MODEL))],
        out_specs=pl.BlockSpec((tm, D_MODEL), lambda i: (i, 0)),
        out_shape=jax.ShapeDtypeStruct((t, D_MODEL), F32),
        compiler_params=_cparams(1),
        name="final_norm",
    )(h1, *([y] * TOP_K), wts, ln_g, ln_b)


def _hi_lo_bf16(w):
    hi = w.astype(BF16)
    lo = (w - hi.astype(F32)).astype(BF16)
    return jnp.concatenate([hi, lo], axis=1)


def _layer_tail(h, attn_o, ssm_o, gates, p, *, tm, moe_rows):
    merged = _merge(attn_o, ssm_o, gates, p["w_branch_attn"], p["w_branch_ssm"], tm=tm, tn=1024)
    h1, ids, wts = _ln1_router(merged, p["w_out"], h, p["ln1_g"], p["ln1_b"],
                               p["w_route"], p["b_route"], tm=tm)
    y = _moe_experts(h1, ids[:, :TOP_K], p["w_gate"], p["w_up"], p["w_down"], bm=moe_rows)
    return _final_norm(h1, y, wts, p["ln2_g"], p["ln2_b"], tm=tm)


def kernel(x_prompt, x_sample, cache_k, cache_v, page_table, state_conv, state_ssm, rel_bias, w_in, lambda_q1, lambda_k1, lambda_q2, lambda_k2, attn_subln, conv_w, conv_b, dt_bias, a_log, d_skip, ssm_norm_w, w_branch_attn, w_branch_ssm, w_out, ln1_g, ln1_b, w_route_group, b_route_group, w_route_expert, b_route_expert, w_gate, w_up, w_down, ln2_g, ln2_b):
    bsz, seq, _ = x_prompt.shape
    n_dec = x_sample.shape[0]
    assert x_sample.shape[1] == 1

    w_main = w_in[:, :OFF_DT].astype(BF16)
    w_dt = jnp.pad(w_in[:, OFF_DT:OFF_GATES], ((0, 0), (0, LANES - SSM_HEADS))).astype(BF16)
    w_gates = w_in[:, OFF_GATES:].astype(BF16)
    lam_rows = jnp.stack([lambda_q1, lambda_k1, lambda_q2, lambda_k2]).astype(F32)
    subln = attn_subln.reshape(1, V_DIM)
    pad_heads = lambda vec: jnp.pad(vec.astype(F32), (0, LANES - SSM_HEADS))
    conv_b2 = conv_b.reshape(1, CONV_DIM)
    dskip_row = jnp.repeat(d_skip.astype(F32), SSM_HEAD_DIM).reshape(1, D_INNER)
    normw_row = ssm_norm_w.reshape(1, D_INNER)
    n_route = MOE_GROUPS + N_EXPERTS
    tail = dict(
        w_branch_attn=w_branch_attn.astype(BF16), w_branch_ssm=w_branch_ssm.astype(BF16),
        w_out=w_out.astype(BF16), ln1_g=ln1_g.reshape(1, D_MODEL), ln1_b=ln1_b.reshape(1, D_MODEL),
        w_route=_hi_lo_bf16(jnp.pad(jnp.concatenate([w_route_group, w_route_expert], axis=1),
                                    ((0, 0), (0, LANES - n_route)))),
        b_route=jnp.pad(jnp.concatenate([b_route_group, b_route_expert]),
                        (0, LANES - n_route)).reshape(1, LANES),
        w_gate=w_gate.astype(BF16), w_up=w_up.astype(BF16), w_down=w_down.astype(BF16),
        ln2_g=ln2_g.reshape(1, D_MODEL), ln2_b=ln2_b.reshape(1, D_MODEL))

    t_p = bsz * seq
    xp = x_prompt.reshape(t_p, D_MODEL)
    q, k, k_bf, v, v_bf, z, xbc, dt, gates = _in_projection(
        xp.astype(BF16), w_main, w_dt, w_gates, tm=1024, q_scale=Q_SCALE * LOG2E)
    attn_p = _prompt_attention(q.reshape(bsz, seq, ATTN_WIDTH), k_bf.reshape(bsz, seq, KV_WIDTH),
                               v_bf.reshape(bsz, seq, KV_WIDTH), rel_bias, lam_rows, subln, tq=512)
    xbc3 = xbc.reshape(bsz, seq, CONV_DIM)
    ssm_p, ssm_state_p = _ssd_prompt(
        xbc3, z.reshape(bsz, seq, D_INNER), dt.reshape(bsz, seq, LANES), conv_w, conv_b2,
        pad_heads(dt_bias).reshape(LANES, 1), pad_heads(a_log).reshape(LANES, 1), dskip_row, normw_row)
    y_prompt = _layer_tail(xp, attn_p.reshape(t_p, ATTN_WIDTH), ssm_p.reshape(t_p, D_INNER), gates,
                           tail, tm=512, moe_rows=MOE_BLOCK).reshape(bsz, seq, D_MODEL)
    k_prompt = k.reshape(bsz, seq, N_KV_HEADS, V_DIM)
    v_prompt = v.reshape(bsz, seq, N_KV_HEADS, V_DIM)
    conv_prompt = xbc3[:, seq - (CONV_WIDTH - 1):, :]
    ssm_prompt = ssm_state_p.reshape(bsz, SSM_HEADS, SSM_HEAD_DIM, D_STATE)

    xs = x_sample.reshape(n_dec, D_MODEL)
    q, k, k_bf, v, v_bf, z, xbc, dt, gates = _in_projection(
        xs.astype(BF16), w_main, w_dt, w_gates, tm=n_dec, q_scale=Q_SCALE)
    attn_s = _decode_attention(q, k, v, cache_k, cache_v, page_table, rel_bias,
                               lam_rows, subln, pages_per_step=8)
    ssm_s, conv_sample, ssm_state_s = _ssd_step(
        xbc, state_conv, z, dt, state_ssm.reshape(n_dec, D_INNER, D_STATE), conv_w, conv_b2,
        pad_heads(dt_bias).reshape(1, LANES), pad_heads(a_log).reshape(1, LANES), dskip_row, normw_row)
    y_sample = _layer_tail(xs, attn_s, ssm_s, gates, tail, tm=n_dec, moe_rows=8).reshape(n_dec, 1, D_MODEL)
    k_sample = k.reshape(n_dec, 1, N_KV_HEADS, V_DIM)
    v_sample = v.reshape(n_dec, 1, N_KV_HEADS, V_DIM)
    ssm_sample = ssm_state_s.reshape(n_dec, SSM_HEADS, SSM_HEAD_DIM, D_STATE)

    return (y_prompt, y_sample, k_prompt, v_prompt, conv_prompt, ssm_prompt,
            k_sample, v_sample, conv_sample, ssm_sample)
```

```python
import functools
import math

import jax
import jax.numpy as jnp
import numpy as np
from jax import lax
from jax.experimental import pallas as pl
from jax.experimental.pallas import tpu as pltpu

F32 = jnp.float32
BF16 = jnp.bfloat16
I32 = jnp.int32

D_MODEL = 2048
DEPTH = 1
N_HEADS = 8
N_KV_HEADS = 4
Q_PER_KV = N_HEADS // N_KV_HEADS
HEAD_DIM = D_MODEL // (2 * N_HEADS)
V_DIM = 2 * HEAD_DIM
ATTN_WIDTH = N_HEADS * V_DIM
KV_WIDTH = N_KV_HEADS * V_DIM
N_BUCKETS = 32
MAX_EXACT = N_BUCKETS // 2
MAX_DISTANCE = 128
D_INNER = 2 * D_MODEL
SSM_HEAD_DIM = 64
SSM_HEADS = D_INNER // SSM_HEAD_DIM
SSM_GROUPS = 8
SSM_HPG = SSM_HEADS // SSM_GROUPS
D_STATE = 128
CONV_WIDTH = 4
GN = SSM_GROUPS * D_STATE
CONV_DIM = D_INNER + 2 * GN
SSD_CHUNK = 128
GROUP_WIDTH = D_INNER // SSM_GROUPS
MOE_GROUPS = 4
EXPERTS_PER_GROUP = 8
N_EXPERTS = MOE_GROUPS * EXPERTS_PER_GROUP
TOP_K = 2
D_FF = D_MODEL // 2
MOE_BLOCK = 128
ALPHA = (2 * DEPTH) ** 0.25
EPS = 1e-5
LAM_INIT = 0.8 - 0.6 * math.exp(-0.3 * 0)
Q_SCALE = HEAD_DIM ** -0.5
LOG2E = math.log2(math.e)
OFF_Q, OFF_K, OFF_V, OFF_Z, OFF_XBC = 0, 2048, 3072, 4096, 8192
OFF_DT = OFF_XBC + CONV_DIM
OFF_GATES = OFF_DT + SSM_HEADS

LANES = 128
V7X_VMEM_LIMIT = 56 * 1024 * 1024


def _cparams(n_axes, vmem=V7X_VMEM_LIMIT):
    return pltpu.CompilerParams(
        dimension_semantics=("arbitrary",) * n_axes, vmem_limit_bytes=vmem)


def _sigmoid(x):
    return 1.0 / (1.0 + jnp.exp(-x))


def _silu(x):
    return x * _sigmoid(x)


def _softplus(x):
    return jnp.maximum(x, 0.0) + jnp.log(1.0 + jnp.exp(-jnp.abs(x)))


def _layer_norm(x, g, b):
    xc = x - jnp.mean(x, axis=-1, keepdims=True)
    var = jnp.mean(xc * xc, axis=-1, keepdims=True)
    return xc * lax.rsqrt(var + EPS) * g + b


def _lane_tile(x, n):
    return jnp.concatenate([x] * n, axis=1)


def _split3(v):
    hi = v.astype(BF16)
    r1 = v - hi.astype(F32)
    mid = r1.astype(BF16)
    lo = (r1 - mid.astype(F32)).astype(BF16)
    return jnp.concatenate([hi, mid, lo], axis=1)


def _proj_kernel(x_ref, w_ref, *o_refs, scale):
    acc = jnp.dot(x_ref[...], w_ref[...], preferred_element_type=F32)
    if scale != 1.0:
        acc = acc * scale
    for o_ref in o_refs:
        o_ref[...] = acc.astype(o_ref.dtype)


def _proj_heads_kernel(x_ref, w_ref, o_ref, obf_ref):
    acc = jnp.dot(x_ref[...], w_ref[...], preferred_element_type=F32)
    for g in range(N_KV_HEADS):
        o_ref[:, g, :] = acc[:, g * V_DIM:(g + 1) * V_DIM]
    obf_ref[...] = acc.astype(BF16)


def _proj_heads(x, w, col0, *, tm, name):
    t, k = x.shape
    assert col0 % KV_WIDTH == 0 and t % tm == 0
    off = col0 // KV_WIDTH
    return pl.pallas_call(
        _proj_heads_kernel,
        grid=(t // tm,),
        in_specs=[pl.BlockSpec((tm, k), lambda i: (i, 0)),
                  pl.BlockSpec((k, KV_WIDTH), lambda i: (0, off))],
        out_specs=[pl.BlockSpec((tm, N_KV_HEADS, V_DIM), lambda i: (i, 0, 0)),
                   pl.BlockSpec((tm, KV_WIDTH), lambda i: (i, 0))],
        out_shape=[jax.ShapeDtypeStruct((t, N_KV_HEADS, V_DIM), F32),
                   jax.ShapeDtypeStruct((t, KV_WIDTH), BF16)],
        compiler_params=_cparams(1),
        name=name,
    )(x, w)


def _proj(x, w, col0, ncols, out_dtypes, *, scale=1.0, tm, tn, name):
    t, k = x.shape
    off = col0 // tn
    assert col0 % tn == 0 and ncols % tn == 0 and t % tm == 0
    return pl.pallas_call(
        functools.partial(_proj_kernel, scale=scale),
        grid=(t // tm, ncols // tn),
        in_specs=[pl.BlockSpec((tm, k), lambda i, j: (i, 0)),
                  pl.BlockSpec((k, tn), lambda i, j: (0, j + off))],
        out_specs=[pl.BlockSpec((tm, tn), lambda i, j: (i, j)) for _ in out_dtypes],
        out_shape=[jax.ShapeDtypeStruct((t, ncols), dt) for dt in out_dtypes],
        compiler_params=_cparams(2),
        name=name,
    )(x, w)


def _in_projection(x_bf, w_main, w_dt, w_gates, tm, q_scale):
    tn = 1024
    (q,) = _proj(x_bf, w_main, OFF_Q, ATTN_WIDTH, [BF16], scale=q_scale, tm=tm, tn=tn, name="proj_q")
    k, k_bf = _proj_heads(x_bf, w_main, OFF_K, tm=tm, name="proj_k")
    v, v_bf = _proj_heads(x_bf, w_main, OFF_V, tm=tm, name="proj_v")
    (z,) = _proj(x_bf, w_main, OFF_Z, D_INNER, [F32], tm=tm, tn=tn, name="proj_z")
    (xbc,) = _proj(x_bf, w_main, OFF_XBC, CONV_DIM, [F32], tm=tm, tn=tn, name="proj_xbc")
    (dt,) = _proj(x_bf, w_dt, 0, LANES, [F32], tm=tm, tn=LANES, name="proj_dt")
    (gates,) = _proj(x_bf, w_gates, 0, 2 * D_MODEL, [F32], tm=tm, tn=tn, name="proj_gates")
    return q, k, k_bf, v, v_bf, z, xbc, dt, gates


def _t5_bucket(dist):
    n = jnp.maximum(dist, 0)
    nf = jnp.maximum(n, 1).astype(F32)
    large = MAX_EXACT + (jnp.log(nf / MAX_EXACT) / math.log(MAX_DISTANCE / MAX_EXACT)
                         * (N_BUCKETS - MAX_EXACT)).astype(I32)
    large = jnp.minimum(large, N_BUCKETS - 1)
    return jnp.where(n < MAX_EXACT, n, large)


def _bias_minus_far(dist, relb_ref, head):
    bucket = _t5_bucket(dist)
    far = relb_ref[N_BUCKETS - 1, head]
    out = jnp.zeros(dist.shape, F32)
    for b in range(N_BUCKETS - 1):
        out = jnp.where(bucket == b, relb_ref[b, head] - far, out)
    return out


def _lambda_value(lam_ref):
    s1 = jnp.sum(lam_ref[0:1, :] * lam_ref[1:2, :], axis=1, keepdims=True)
    s2 = jnp.sum(lam_ref[2:3, :] * lam_ref[3:4, :], axis=1, keepdims=True)
    return jnp.exp(s1) - jnp.exp(s2) + LAM_INIT


def _subln(o, subln_ref):
    ms = jnp.mean(o * o, axis=-1, keepdims=True)
    return o * lax.rsqrt(ms + EPS) * subln_ref[...] * (1.0 - LAM_INIT)


def _prompt_attn_kernel(relb_ref, lam_ref, subln_ref, q_ref, k_ref, v_ref, o_ref,
                        qs_ref, m_ref, l_ref, acc_ref, bias_ref, *, tq):
    g = pl.program_id(0)
    b = pl.program_id(1)
    qi = pl.program_id(2)
    rows = 2 * tq

    @pl.when(jnp.logical_and(b == 0, qi == 0))
    def _():
        r = lax.broadcasted_iota(I32, (tq, tq), 0)
        c = lax.broadcasted_iota(I32, (tq, tq), 1)
        for hh in range(Q_PER_KV):
            head = g * Q_PER_KV + hh
            adj = _bias_minus_far(tq + r - c, relb_ref, head) * LOG2E
            diag = _bias_minus_far(r - c, relb_ref, head) * LOG2E
            diag = jnp.where(r >= c, diag, -jnp.inf)
            bias_ref[0, hh * tq:(hh + 1) * tq, :] = adj
            bias_ref[1, hh * tq:(hh + 1) * tq, :] = diag

    for hh in range(Q_PER_KV):
        for mm in range(2):
            c0 = hh * V_DIM + mm * HEAD_DIM
            qs_ref[mm, hh * tq:(hh + 1) * tq, :] = q_ref[0, :, c0:c0 + HEAD_DIM]

    m_ref[...] = jnp.full(m_ref.shape, -jnp.inf, F32)
    l_ref[...] = jnp.zeros(l_ref.shape, F32)
    acc_ref[...] = jnp.zeros(acc_ref.shape, F32)

    chains = [(mm, hh) for mm in range(2) for hh in range(Q_PER_KV)]

    def key_tile_start(ki):
        return pl.multiple_of(ki * tq, tq)

    def chain_update(mm, hh, ki, bias_kind):
        rs = slice(hh * tq, (hh + 1) * tq)
        kt = k_ref[0, pl.ds(key_tile_start(ki), tq), mm * HEAD_DIM:(mm + 1) * HEAD_DIM]
        vt = v_ref[0, pl.ds(key_tile_start(ki), tq), :]
        s = lax.dot_general(qs_ref[mm, rs, :], kt, (((1,), (1,)), ((), ())),
                            preferred_element_type=F32)
        if bias_kind is not None:
            s = s + bias_ref[bias_kind, rs, :]
        m_prev = m_ref[mm, rs, :]
        m_new = jnp.maximum(m_prev, jnp.max(s, axis=1, keepdims=True))
        p = jnp.exp2(s - _lane_tile(m_new, tq // LANES))
        alpha = jnp.exp2(m_prev - m_new)
        l_ref[mm, rs, :] = alpha * l_ref[mm, rs, :] + jnp.sum(p, axis=1, keepdims=True)
        acc_ref[mm, rs, :] = (acc_ref[mm, rs, :] * _lane_tile(alpha, V_DIM // LANES)
                              + jnp.dot(p.astype(BF16), vt, preferred_element_type=F32))
        m_ref[mm, rs, :] = m_new

    def tile_update(ki, bias_kind):
        for mm, hh in chains:
            chain_update(mm, hh, ki, bias_kind)

    n_far = jnp.maximum(qi - 1, 0)

    def far_pair(kk, carry):
        tile_update(2 * kk, None)
        tile_update(2 * kk + 1, None)
        return carry
    lax.fori_loop(0, n_far // 2, far_pair, 0)

    @pl.when(n_far % 2 == 1)
    def _():
        tile_update(n_far - 1, None)

    @pl.when(qi >= 1)
    def _():
        tile_update(qi - 1, 0)
        tile_update(qi, 1)

    @pl.when(qi == 0)
    def _():
        tile_update(0, 1)

    lam = _lambda_value(lam_ref)
    o1 = acc_ref[0] / _lane_tile(l_ref[0], V_DIM // LANES)
    o2 = acc_ref[1] / _lane_tile(l_ref[1], V_DIM // LANES)
    o = _subln(o1 - lam * o2, subln_ref)
    for hh in range(Q_PER_KV):
        o_ref[0, :, hh * V_DIM:(hh + 1) * V_DIM] = o[hh * tq:(hh + 1) * tq].astype(o_ref.dtype)


def _prompt_attention(q, k_bf, v_bf, rel_bias, lam_rows, subln, *, tq):
    bsz, length, _ = q.shape
    assert length % tq == 0 and tq % LANES == 0
    gw = Q_PER_KV * V_DIM
    rows = 2 * tq
    return pl.pallas_call(
        functools.partial(_prompt_attn_kernel, tq=tq),
        grid=(N_KV_HEADS, bsz, length // tq),
        in_specs=[
            pl.BlockSpec(memory_space=pltpu.SMEM),
            pl.BlockSpec((4, HEAD_DIM), lambda g, b, i: (0, 0)),
            pl.BlockSpec((1, V_DIM), lambda g, b, i: (0, 0)),
            pl.BlockSpec((1, tq, gw), lambda g, b, i: (b, i, g)),
            pl.BlockSpec((1, length, V_DIM), lambda g, b, i: (b, 0, g)),
            pl.BlockSpec((1, length, V_DIM), lambda g, b, i: (b, 0, g)),
        ],
        out_specs=pl.BlockSpec((1, tq, gw), lambda g, b, i: (b, i, g)),
        out_shape=jax.ShapeDtypeStruct((bsz, length, ATTN_WIDTH), BF16),
        scratch_shapes=[
            pltpu.VMEM((2, rows, HEAD_DIM), BF16),
            pltpu.VMEM((2, rows, LANES), F32),
            pltpu.VMEM((2, rows, LANES), F32),
            pltpu.VMEM((2, rows, V_DIM), F32),
            pltpu.VMEM((2, rows, tq), F32),
        ],
        compiler_params=_cparams(3),
        name="prompt_attention",
    )(rel_bias, lam_rows, subln, q, k_bf, v_bf)


def _decode_attn_kernel(pt_ref, relb_ref, lam_ref, subln_ref, q_ref, kn_ref, vn_ref, *rest,
                        pages_per_step, page, past_len):
    del pt_ref
    kp = rest[:pages_per_step]
    vp = rest[pages_per_step:2 * pages_per_step]
    o_ref, m_ref, l_ref, acc_ref = rest[2 * pages_per_step:]
    j = pl.program_id(1)
    n_steps = pl.num_programs(1)
    rpt = 2 * N_KV_HEADS
    prow = page * rpt
    nq = 2 * N_HEADS

    @pl.when(j == 0)
    def _():
        m_ref[...] = jnp.full(m_ref.shape, -jnp.inf, F32)
        l_ref[...] = jnp.zeros(l_ref.shape, F32)
        acc_ref[...] = jnp.zeros(acc_ref.shape, F32)

    q_all = q_ref[0]

    def row_head(shape):
        r = lax.broadcasted_iota(I32, shape, 0)
        return (r % N_KV_HEADS) * Q_PER_KV + (r // N_KV_HEADS) % Q_PER_KV

    def row_bias(dist):
        out = jnp.zeros(dist.shape, F32)
        head = row_head(dist.shape)
        for h in range(N_HEADS):
            out = jnp.where(head == h, _bias_minus_far(dist, relb_ref, h), out)
        return out

    def online(s, pv_fn):
        m_prev = m_ref[...]
        m_new = jnp.maximum(m_prev, jnp.max(s, axis=1, keepdims=True))
        p = jnp.exp(s - m_new[:, :1])
        alpha = jnp.exp(m_prev - m_new)
        l_ref[...] = alpha * l_ref[...] + jnp.sum(p, axis=1, keepdims=True)
        acc_ref[...] = acc_ref[...] * jnp.concatenate([alpha, alpha], axis=0) + pv_fn(p)
        m_ref[...] = m_new

    s_parts = [lax.dot_general(q_all, kp[t][0].astype(BF16), (((1,), (1,)), ((), ())),
                               preferred_element_type=F32) for t in range(pages_per_step)]

    def add_near_bias(s_last):
        c = lax.broadcasted_iota(I32, (nq, prow), 1)
        pos = (j * pages_per_step + pages_per_step - 1) * page + c // rpt
        return s_last + row_bias(past_len - pos)

    s_parts[-1] = lax.cond(j == n_steps - 1, add_near_bias, lambda s: s, s_parts[-1])
    s = jnp.concatenate(s_parts, axis=1)
    col = lax.broadcasted_iota(I32, s.shape, 1)
    row = lax.broadcasted_iota(I32, s.shape, 0)
    own = col % rpt == (row // (nq // 2)) * N_KV_HEADS + row % N_KV_HEADS
    s = jnp.where(own, s, -jnp.inf)

    def pv(p):
        width = p.shape[1]
        up = pltpu.roll(p, N_KV_HEADS, axis=1)
        down = pltpu.roll(p, width - N_KV_HEADS, axis=1)
        top = row[:, :1] < nq // 2
        p_half0 = jnp.where(top, p, down)
        p_half1 = jnp.where(top, up, p)
        lhs = jnp.concatenate([p_half0, p_half1], axis=0).astype(BF16)
        out = jnp.zeros((2 * nq, LANES), F32)
        for t in range(pages_per_step):
            out = out + jnp.dot(lhs[:, t * prow:(t + 1) * prow], vp[t][0].astype(BF16),
                                preferred_element_type=F32)
        return out

    online(s, pv)

    @pl.when(j == n_steps - 1)
    def _():
        def per_row(ref, part):
            blk = ref[0, :, part * LANES:(part + 1) * LANES].astype(BF16).astype(F32)
            return jnp.concatenate([blk] * Q_PER_KV, axis=0)

        kn = jnp.concatenate([per_row(kn_ref, 0), per_row(kn_ref, 1)], axis=0)
        s_new = jnp.sum(q_all.astype(F32) * kn, axis=1, keepdims=True)
        s_new = s_new + row_bias(jnp.zeros((nq, 1), I32))
        vn = jnp.concatenate([per_row(vn_ref, 0)] * 2 + [per_row(vn_ref, 1)] * 2, axis=0)
        online(s_new, lambda p: jnp.concatenate([p, p], axis=0).astype(BF16).astype(F32) * vn)

        lam = _lambda_value(lam_ref)
        l_all = l_ref[...]
        o_lo = acc_ref[0:nq] / l_all
        o_hi = acc_ref[nq:2 * nq] / l_all
        o_all = jnp.concatenate([o_lo, o_hi], axis=1)
        o = _subln(o_all[0:nq // 2] - lam * o_all[nq // 2:nq], subln_ref)
        for hh in range(Q_PER_KV):
            for g in range(N_KV_HEADS):
                h = g * Q_PER_KV + hh
                r = hh * N_KV_HEADS + g
                o_ref[0, :, h * V_DIM:(h + 1) * V_DIM] = o[r:r + 1].astype(o_ref.dtype)


def _decode_attention(q, k_new, v_new, cache_k, cache_v, page_table, rel_bias, lam_rows, subln,
                      *, pages_per_step):
    n_seq = q.shape[0]
    page = cache_k.shape[1]
    n_pages = page_table.shape[1]
    assert n_pages % pages_per_step == 0
    past_len = n_pages * page
    q3 = (q.reshape(n_seq, N_KV_HEADS, Q_PER_KV, 2, HEAD_DIM).transpose(0, 3, 2, 1, 4)
          .reshape(n_seq, N_HEADS * 2, HEAD_DIM))
    prow = page * 2 * N_KV_HEADS

    def stored_rows(cache):
        return (cache.reshape(-1, page, N_KV_HEADS, 2, LANES).transpose(0, 1, 3, 2, 4)
                .reshape(-1, prow, LANES))

    def page_spec(t):
        return pl.BlockSpec((1, prow, LANES),
                            lambda i, j, pt, t=t: (pt[i, j * pages_per_step + t], 0, 0))

    grid_spec = pltpu.PrefetchScalarGridSpec(
        num_scalar_prefetch=1,
        grid=(n_seq, n_pages // pages_per_step),
        in_specs=[
            pl.BlockSpec(memory_space=pltpu.SMEM),
            pl.BlockSpec((4, HEAD_DIM), lambda i, j, pt: (0, 0)),
            pl.BlockSpec((1, V_DIM), lambda i, j, pt: (0, 0)),
            pl.BlockSpec((1, N_HEADS * 2, HEAD_DIM), lambda i, j, pt: (i, 0, 0)),
            pl.BlockSpec((1, N_KV_HEADS, V_DIM), lambda i, j, pt: (i, 0, 0)),
            pl.BlockSpec((1, N_KV_HEADS, V_DIM), lambda i, j, pt: (i, 0, 0)),
        ] + [page_spec(t) for t in range(pages_per_step)] * 2,
        out_specs=pl.BlockSpec((1, 1, ATTN_WIDTH), lambda i, j, pt: (i, 0, 0)),
        scratch_shapes=[
            pltpu.VMEM((2 * N_HEADS, LANES), F32),
            pltpu.VMEM((2 * N_HEADS, LANES), F32),
            pltpu.VMEM((4 * N_HEADS, LANES), F32),
        ],
    )
    out = pl.pallas_call(
        functools.partial(_decode_attn_kernel, pages_per_step=pages_per_step, page=page,
                          past_len=past_len),
        grid_spec=grid_spec,
        out_shape=jax.ShapeDtypeStruct((n_seq, 1, ATTN_WIDTH), BF16),
        compiler_params=_cparams(2),
        name="decode_attention",
    )(page_table, rel_bias, lam_rows, subln, q3, k_new, v_new,
      *([stored_rows(cache_k)] * pages_per_step), *([stored_rows(cache_v)] * pages_per_step))
    return out.reshape(n_seq, ATTN_WIDTH)


def _gate_and_norm(y, z, normw):
    y = y * _silu(z)
    parts = []
    for g in range(SSM_GROUPS):
        yg = y[:, g * GROUP_WIDTH:(g + 1) * GROUP_WIDTH]
        ms = jnp.mean(yg * yg, axis=-1, keepdims=True)
        parts.append(yg * lax.rsqrt(ms + EPS))
    return jnp.concatenate(parts, axis=1) * normw


def _ssd_prompt_kernel(xbc_ref, z_ref, dt_ref, convw_ref, convb_ref, dtb_ref, alog_ref,
                       dskip_ref, normw_ref, o_ref, state_ref,
                       ext_ref, act_ref, st_ref, y_ref, e3_ref, *, chunk):
    c = pl.program_id(1)
    n_chunks = pl.num_programs(1)
    halo = CONV_WIDTH - 1
    base = 8
    pair_w = 2 * SSM_HEAD_DIM

    @pl.when(c == 0)
    def _():
        ext_ref[0:base, :] = jnp.zeros((base, CONV_DIM), F32)
        st_ref[...] = jnp.zeros(st_ref.shape, F32)
        r = lax.broadcasted_iota(I32, e3_ref.shape, 0) % LANES
        col = lax.broadcasted_iota(I32, e3_ref.shape, 1)
        e3_ref[...] = jnp.where(col // SSM_HEAD_DIM == r, 1.0, 0.0).astype(BF16)

    ext_ref[base:base + chunk, :] = xbc_ref[0]
    col_tile = 512
    for j in range(CONV_DIM // col_tile):
        cs = slice(j * col_tile, (j + 1) * col_tile)
        conv = convb_ref[:, cs] + jnp.zeros((chunk, col_tile), F32)
        for w in range(CONV_WIDTH):
            lo = base - halo + w
            conv = conv + ext_ref[lo:lo + chunk, cs] * convw_ref[w:w + 1, cs]
        act_ref[:, cs] = _silu(conv)
    ext_ref[base - halo:base, :] = ext_ref[base + chunk - halo:base + chunk, :]

    dt_t = _softplus(jnp.transpose(dt_ref[0]) + dtb_ref[...])
    a_t = dt_t * (-jnp.exp(alog_ref[...]))
    tt = lax.broadcasted_iota(I32, (chunk, chunk), 0)
    ss = lax.broadcasted_iota(I32, (chunk, chunk), 1)
    upper = jnp.where(tt <= ss, 1.0, 0.0).astype(BF16)
    upper3 = jnp.concatenate([upper, upper, upper], axis=0)
    acs_t = jnp.dot(_split3(a_t), upper3, preferred_element_type=F32)
    acs = jnp.transpose(acs_t)
    w_t = dt_t * jnp.exp(acs_t[:, chunk - 1:chunk] - acs_t)
    eexp = jnp.dot(_split3(jnp.exp(acs)), e3_ref[...],
                   preferred_element_type=F32)
    causal = tt >= ss

    for g in range(SSM_GROUPS):
        bg = act_ref[:, D_INNER + g * D_STATE:D_INNER + (g + 1) * D_STATE]
        cg = act_ref[:, D_INNER + GN + g * D_STATE:D_INNER + GN + (g + 1) * D_STATE]
        bg_bf = bg.astype(BF16)
        cg_bf = cg.astype(BF16)
        cb = lax.dot_general(cg_bf, bg_bf, (((1,), (1,)), ((), ())),
                             preferred_element_type=F32)
        bg_t = jnp.transpose(bg)
        gsl = slice(g * GROUP_WIDTH, (g + 1) * GROUP_WIDTH)
        y_off = jnp.dot(cg_bf, st_ref[:, gsl].astype(BF16), preferred_element_type=F32)
        for pr in range(SSM_HPG // 2):
            h0 = g * SSM_HPG + 2 * pr
            psl = slice(h0 * SSM_HEAD_DIM, h0 * SSM_HEAD_DIM + pair_w)
            lhs_d, lhs_s = [], []
            for h in (h0, h0 + 1):
                seg = acs[:, h:h + 1] - acs_t[h:h + 1, :]
                dec = jnp.exp(jnp.where(causal, seg, -jnp.inf))
                lhs_d.append((cb * dec * dt_t[h:h + 1, :]).astype(BF16))
                lhs_s.append((bg_t * w_t[h:h + 1, :]).astype(BF16))
            lhs = jnp.concatenate([jnp.concatenate(lhs_d, axis=1),
                                   jnp.concatenate(lhs_s, axis=1)], axis=0)
            xs = act_ref[:, psl]
            lane = lax.broadcasted_iota(I32, xs.shape, 1)
            rhs = jnp.concatenate([jnp.where(lane < SSM_HEAD_DIM, xs, 0.0),
                                   jnp.where(lane >= SSM_HEAD_DIM, xs, 0.0)], axis=0).astype(BF16)
            res = jnp.dot(lhs, rhs, preferred_element_type=F32)
            y_ref[:, psl] = (res[:chunk] + y_off[:, 2 * pr * SSM_HEAD_DIM:2 * pr * SSM_HEAD_DIM + pair_w]
                             * eexp[:, psl] + dskip_ref[:, psl] * xs)
            st_ref[:, psl] = st_ref[:, psl] * eexp[chunk - 1:chunk, psl] + res[chunk:]

    o_ref[0] = _gate_and_norm(y_ref[...], z_ref[0], normw_ref[...]).astype(o_ref.dtype)

    @pl.when(c == n_chunks - 1)
    def _():
        for j in range(D_INNER // LANES):
            state_ref[0, j * LANES:(j + 1) * LANES, :] = jnp.transpose(
                st_ref[:, j * LANES:(j + 1) * LANES])


def _ssd_prompt(xbc, z, dt, conv_w, conv_b, dtb_col, alog_col, dskip_row, norm_w):
    bsz, length, _ = xbc.shape
    chunk = SSD_CHUNK
    assert length % chunk == 0

    def const(shape):
        return pl.BlockSpec(shape, lambda b, c: (0,) * len(shape))

    return pl.pallas_call(
        functools.partial(_ssd_prompt_kernel, chunk=chunk),
        grid=(bsz, length // chunk),
        in_specs=[
            pl.BlockSpec((1, chunk, CONV_DIM), lambda b, c: (b, c, 0)),
            pl.BlockSpec((1, chunk, D_INNER), lambda b, c: (b, c, 0)),
            pl.BlockSpec((1, chunk, LANES), lambda b, c: (b, c, 0)),
            const((CONV_WIDTH, CONV_DIM)), const((1, CONV_DIM)),
            const((LANES, 1)), const((LANES, 1)),
            const((1, D_INNER)), const((1, D_INNER)),
        ],
        out_specs=[
            pl.BlockSpec((1, chunk, D_INNER), lambda b, c: (b, c, 0)),
            pl.BlockSpec((1, D_INNER, D_STATE), lambda b, c: (b, 0, 0)),
        ],
        out_shape=[
            jax.ShapeDtypeStruct((bsz, length, D_INNER), BF16),
            jax.ShapeDtypeStruct((bsz, D_INNER, D_STATE), F32),
        ],
        scratch_shapes=[
            pltpu.VMEM((8 + chunk, CONV_DIM), F32),
            pltpu.VMEM((chunk, CONV_DIM), F32),
            pltpu.VMEM((D_STATE, D_INNER), F32),
            pltpu.VMEM((chunk, D_INNER), F32),
            pltpu.VMEM((3 * LANES, D_INNER), BF16),
        ],
        compiler_params=_cparams(2),
        name="ssd_prompt",
    )(xbc, z, dt, conv_w, conv_b, dtb_col, alog_col, dskip_row, norm_w)


def _row_to_col(row):
    r = lax.broadcasted_iota(I32, (LANES, LANES), 0)
    c = lax.broadcasted_iota(I32, (LANES, LANES), 1)
    return jnp.sum(jnp.where(r == c, jnp.broadcast_to(row, (LANES, LANES)), 0.0),
                   axis=1, keepdims=True)


def _ssd_step_kernel(xbc_ref, sconv_ref, z_ref, dt_ref, state_ref, convw_ref, convb_ref,
                     dtb_ref, alog_ref, dskip_ref, normw_ref,
                     o_ref, conv_out_ref, state_out_ref, e3_ref):
    halo = CONV_WIDTH - 1

    @pl.when(pl.program_id(0) == 0)
    def _():
        r = lax.broadcasted_iota(I32, e3_ref.shape, 0) % LANES
        col = lax.broadcasted_iota(I32, e3_ref.shape, 1)
        e3_ref[...] = jnp.where(col // SSM_HEAD_DIM == r, 1.0, 0.0).astype(BF16)

    x_new = xbc_ref[0]
    conv = convb_ref[...] + x_new * convw_ref[halo:halo + 1, :]
    for w in range(halo):
        conv = conv + sconv_ref[0, w:w + 1, :] * convw_ref[w:w + 1, :]
    act = _silu(conv)
    conv_out_ref[0, 0:halo - 1, :] = sconv_ref[0, 1:halo, :]
    conv_out_ref[0, halo - 1:halo, :] = x_new

    dt = _softplus(dt_ref[0] + dtb_ref[...])
    decay = jnp.exp(dt * (-jnp.exp(alog_ref[...])))
    both = jnp.concatenate([decay, dt, jnp.zeros((6, LANES), F32)], axis=0)
    expanded = jnp.dot(_split3(both), e3_ref[...], preferred_element_type=F32)
    decay_x = expanded[0:1]
    xs = act[:, :D_INNER]
    xdt = xs * expanded[1:2]

    y_parts = []
    for g in range(SSM_GROUPS):
        b_row = act[:, D_INNER + g * D_STATE:D_INNER + (g + 1) * D_STATE]
        c_row = act[:, D_INNER + GN + g * D_STATE:D_INNER + GN + (g + 1) * D_STATE]
        for k in range(GROUP_WIDTH // LANES):
            r0 = g * GROUP_WIDTH + k * LANES
            new = (state_ref[0, r0:r0 + LANES, :] * _row_to_col(decay_x[:, r0:r0 + LANES])
                   + _row_to_col(xdt[:, r0:r0 + LANES]) * b_row)
            state_out_ref[0, r0:r0 + LANES, :] = new
        h_new = state_out_ref[0, g * GROUP_WIDTH:(g + 1) * GROUP_WIDTH, :].astype(BF16)
        c8 = jnp.broadcast_to(c_row, (8, D_STATE)).astype(BF16)
        y_parts.append(lax.dot_general(c8, h_new, (((1,), (1,)), ((), ())),
                                       preferred_element_type=F32)[0:1])
    y = jnp.concatenate(y_parts, axis=1) + dskip_ref[...] * xs
    o_ref[0] = _gate_and_norm(y, z_ref[0], normw_ref[...]).astype(o_ref.dtype)


def _ssd_step(xbc, state_conv, z, dt, state_ssm, conv_w, conv_b, dtb_row, alog_row, dskip_row, norm_w):
    n_seq = xbc.shape[0]
    halo = CONV_WIDTH - 1

    def const(shape):
        return pl.BlockSpec(shape, lambda i: (0,) * len(shape))

    y, conv_out, state_out = pl.pallas_call(
        _ssd_step_kernel,
        grid=(n_seq,),
        in_specs=[
            pl.BlockSpec((1, 1, CONV_DIM), lambda i: (i, 0, 0)),
            pl.BlockSpec((1, halo, CONV_DIM), lambda i: (i, 0, 0)),
            pl.BlockSpec((1, 1, D_INNER), lambda i: (i, 0, 0)),
            pl.BlockSpec((1, 1, LANES), lambda i: (i, 0, 0)),
            pl.BlockSpec((1, D_INNER, D_STATE), lambda i: (i, 0, 0)),
            const((CONV_WIDTH, CONV_DIM)), const((1, CONV_DIM)),
            const((1, LANES)), const((1, LANES)),
            const((1, D_INNER)), const((1, D_INNER)),
        ],
        out_specs=[
            pl.BlockSpec((1, 1, D_INNER), lambda i: (i, 0, 0)),
            pl.BlockSpec((1, halo, CONV_DIM), lambda i: (i, 0, 0)),
            pl.BlockSpec((1, D_INNER, D_STATE), lambda i: (i, 0, 0)),
        ],
        out_shape=[
            jax.ShapeDtypeStruct((n_seq, 1, D_INNER), BF16),
            jax.ShapeDtypeStruct((n_seq, halo, CONV_DIM), F32),
            jax.ShapeDtypeStruct((n_seq, D_INNER, D_STATE), F32),
        ],
        scratch_shapes=[pltpu.VMEM((3 * LANES, D_INNER), BF16)],
        compiler_params=_cparams(1),
        name="ssd_step",
    )(xbc.reshape(n_seq, 1, CONV_DIM), state_conv, z.reshape(n_seq, 1, D_INNER),
      dt.reshape(n_seq, 1, LANES), state_ssm, conv_w, conv_b, dtb_row, alog_row, dskip_row, norm_w)
    return y.reshape(n_seq, D_INNER), conv_out, state_out


def _merge_kernel(attn_ref, ssm_ref, wa_ref, ws_ref, ga_ref, gb_ref, o_ref):
    a = jnp.dot(attn_ref[...], wa_ref[...], preferred_element_type=F32)
    s = jnp.dot(ssm_ref[...], ws_ref[...], preferred_element_type=F32)
    o_ref[...] = (_sigmoid(ga_ref[...]) * a + _sigmoid(gb_ref[...]) * s).astype(o_ref.dtype)


def _merge(attn_o, ssm_o, gates, w_attn, w_ssm, *, tm, tn):
    t = attn_o.shape[0]
    nb = D_MODEL // tn
    return pl.pallas_call(
        _merge_kernel,
        grid=(t // tm, nb),
        in_specs=[
            pl.BlockSpec((tm, ATTN_WIDTH), lambda i, j: (i, 0)),
            pl.BlockSpec((tm, D_INNER), lambda i, j: (i, 0)),
            pl.BlockSpec((ATTN_WIDTH, tn), lambda i, j: (0, j)),
            pl.BlockSpec((D_INNER, tn), lambda i, j: (0, j)),
            pl.BlockSpec((tm, tn), lambda i, j: (i, j)),
            pl.BlockSpec((tm, tn), lambda i, j: (i, j + nb)),
        ],
        out_specs=pl.BlockSpec((tm, tn), lambda i, j: (i, j)),
        out_shape=jax.ShapeDtypeStruct((t, D_MODEL), BF16),
        compiler_params=_cparams(2),
        name="branch_merge",
    )(attn_o, ssm_o, w_attn, w_ssm, gates, gates)


def _ln1_router_kernel(m_ref, wout_ref, h_ref, g_ref, b_ref, wr_ref, br_ref,
                       h1_ref, ids_ref, wts_ref):
    x = ALPHA * h_ref[...] + jnp.dot(m_ref[...], wout_ref[...], preferred_element_type=F32)
    h1 = _layer_norm(x, g_ref[...], b_ref[...])
    h1_ref[...] = h1

    h_hi = h1.astype(BF16)
    h_lo = (h1 - h_hi.astype(F32)).astype(BF16)
    by_hi = jnp.dot(h_hi, wr_ref[...], preferred_element_type=F32)
    logits = (by_hi[:, :LANES] + by_hi[:, LANES:]
              + jnp.dot(h_lo, wr_ref[:, :LANES], preferred_element_type=F32) + br_ref[...])
    lane = lax.broadcasted_iota(I32, logits.shape, 1)
    big = jnp.int32(LANES)
    neg = -jnp.inf
    gl = jnp.where(lane < MOE_GROUPS, logits, neg)
    gmax = jnp.max(gl, axis=1, keepdims=True)
    g_idx = jnp.min(jnp.where(gl == gmax, lane, big), axis=1, keepdims=True)
    g_w = 1.0 / jnp.sum(jnp.exp(gl - gmax), axis=1, keepdims=True)
    e_lo = MOE_GROUPS + g_idx * EXPERTS_PER_GROUP
    el = jnp.where(jnp.logical_and(lane >= e_lo, lane < e_lo + EXPERTS_PER_GROUP), logits, neg)
    e1 = jnp.max(el, axis=1, keepdims=True)
    i1 = jnp.min(jnp.where(el == e1, lane, big), axis=1, keepdims=True)
    el2 = jnp.where(lane == i1, neg, el)
    e2 = jnp.max(el2, axis=1, keepdims=True)
    i2 = jnp.min(jnp.where(el2 == e2, lane, big), axis=1, keepdims=True)
    t2 = jnp.exp(e2 - e1)
    den = 1.0 + t2
    w1 = g_w * (1.0 / den)
    w2 = g_w * (t2 / den)
    ids_ref[...] = jnp.where(lane == 0, i1 - MOE_GROUPS, jnp.where(lane == 1, i2 - MOE_GROUPS, 0))
    wts_ref[...] = jnp.where(lane == 0, w1, jnp.where(lane == 1, w2, 0.0))


def _ln1_router(merged, w_out, h, ln_g, ln_b, w_route, b_route, *, tm):
    t = merged.shape[0]

    def const(shape):
        return pl.BlockSpec(shape, lambda i: (0,) * len(shape))

    def rows(width):
        return pl.BlockSpec((tm, width), lambda i: (i, 0))

    return pl.pallas_call(
        _ln1_router_kernel,
        grid=(t // tm,),
        in_specs=[rows(D_MODEL), const((D_MODEL, D_MODEL)), rows(D_MODEL),
                  const((1, D_MODEL)), const((1, D_MODEL)),
                  const((D_MODEL, 2 * LANES)), const((1, LANES))],
        out_specs=[rows(D_MODEL), rows(LANES), rows(LANES)],
        out_shape=[jax.ShapeDtypeStruct((t, D_MODEL), F32),
                   jax.ShapeDtypeStruct((t, LANES), I32),
                   jax.ShapeDtypeStruct((t, LANES), F32)],
        compiler_params=_cparams(1),
        name="ln1_router",
    )(merged, w_out, h, ln_g, ln_b, w_route, b_route)


MOE_X_SLOTS = 3
MOE_Y_SLOTS = 2
MOE_DRAIN_STEPS = 2


def _moe_kernel(be_ref, nused_ref, g0_ref, g1_ref, g2_ref, sidx_ref, x_hbm, wg_ref, wu_ref, wd_ref,
                y_hbm, xbuf, ybuf, zrow, gsem, ssem, *, bm):
    del be_ref
    i = pl.program_id(0)
    n_used = nused_ref[0]
    xs = i % MOE_X_SLOTS
    ys = i % MOE_Y_SLOTS
    slab = y_hbm.shape[0] // TOP_K

    def gather_start(idx_ref, slot):
        for j in range(bm):
            pltpu.make_async_copy(x_hbm.at[pl.ds(idx_ref[0, 0, j], 1), :],
                                  xbuf.at[slot, pl.ds(j, 1), :], gsem.at[slot]).start(priority=j % 2)

    def gather_wait(slot):
        for j in range(bm):
            pltpu.make_async_copy(x_hbm.at[pl.ds(0, 1), :],
                                  xbuf.at[slot, pl.ds(j, 1), :], gsem.at[slot]).wait()

    def scatter_start(slot):
        for j in range(bm):
            pltpu.make_async_copy(ybuf.at[slot, pl.ds(j, 1), :],
                                  y_hbm.at[pl.ds(sidx_ref[0, 0, j], 1), :], ssem.at[slot]).start(priority=j % 2)

    def result_slot_wait(slot):
        for j in range(bm):
            pltpu.make_async_copy(ybuf.at[slot, pl.ds(j, 1), :],
                                  y_hbm.at[pl.ds(0, 1), :], ssem.at[slot]).wait()

    @pl.when(i == 0)
    def _():
        zrow[...] = jnp.zeros(zrow.shape, F32)
        for s in range(MOE_Y_SLOTS):
            for j in range(bm):
                pltpu.make_async_copy(zrow, y_hbm.at[pl.ds((s + 1) * slab - bm + j, 1), :],
                                      ssem.at[s]).start()
        gather_start(g0_ref, 0)
        gather_start(g1_ref, 1)

    @pl.when(i < n_used)
    def _():
        result_slot_wait(ys)
        gather_wait(xs)
        x = xbuf[xs].astype(BF16)
        gate = jnp.dot(x, wg_ref[0], preferred_element_type=F32)
        up = jnp.dot(x, wu_ref[0], preferred_element_type=F32)
        gather_start(g2_ref, (i + 2) % MOE_X_SLOTS)
        hmid = (_silu(gate) * up).astype(BF16)
        ybuf[ys] = jnp.dot(hmid, wd_ref[0], preferred_element_type=F32)
        scatter_start(ys)

    @pl.when(jnp.logical_and(i >= n_used, i < n_used + MOE_DRAIN_STEPS))
    def _():
        result_slot_wait(ys)
        gather_wait(xs)


def _moe_experts(x, ids, w_gate, w_up, w_down, *, bm):
    t = x.shape[0]
    n_assign = t * TOP_K
    n_blocks = -(-(n_assign + N_EXPERTS * (bm - 1)) // bm) + MOE_DRAIN_STEPS
    rows = n_blocks * bm
    flat = ids.reshape(-1)
    order = jnp.argsort(flat, stable=True).astype(I32)
    experts = jnp.arange(N_EXPERTS, dtype=I32)
    counts = jnp.sum((flat[:, None] == experts[None, :]).astype(I32), axis=0)
    starts = jnp.cumsum(counts) - counts
    padded = (counts + bm - 1) // bm * bm
    pad_ends = jnp.cumsum(padded)
    pad_starts = pad_ends - padded
    n_used = (pad_ends[-1] // bm).astype(I32)
    blk = jnp.arange(n_blocks, dtype=I32)
    owner = jnp.minimum(jnp.sum((pad_ends[None, :] <= (blk * bm)[:, None]).astype(I32), axis=1),
                        N_EXPERTS - 1)
    last_used = owner[jnp.maximum(n_used - 1, 0)]
    block_expert = jnp.where(blk < n_used, owner, last_used)
    pos = jnp.arange(rows, dtype=I32)
    spread = lambda per_block: jnp.repeat(per_block, bm)
    off = pos - spread(pad_starts[owner])
    valid = jnp.logical_and(pos < pad_ends[-1], off < spread(counts[owner]))
    src = jnp.clip(spread(starts[owner]) + off, 0, n_assign - 1)
    row_assign = jnp.where(valid, order[src], n_assign)
    gather_idx = jnp.where(valid, row_assign // TOP_K, 0).reshape(n_blocks, 1, bm)
    assert TOP_K == 2
    slab = t + bm
    dump = ((pos // bm) % MOE_Y_SLOTS) * slab + t + pos % bm
    dest = (row_assign % TOP_K) * slab + row_assign // TOP_K
    scatter_idx = jnp.where(valid, dest, dump).reshape(n_blocks, 1, bm)

    def idx_spec(shift):
        return pl.BlockSpec((1, 1, bm), lambda i, be, nu: (jnp.minimum(i + shift, n_blocks - 1), 0, 0),
                            memory_space=pltpu.SMEM)

    def w_spec(shape):
        return pl.BlockSpec((1,) + shape, lambda i, be, nu: (be[i], 0, 0))

    grid_spec = pltpu.PrefetchScalarGridSpec(
        num_scalar_prefetch=2,
        grid=(n_blocks,),
        in_specs=[idx_spec(0), idx_spec(1), idx_spec(2), idx_spec(0),
                  pl.BlockSpec(memory_space=pl.ANY),
                  w_spec((D_MODEL, D_FF)), w_spec((D_MODEL, D_FF)), w_spec((D_FF, D_MODEL))],
        out_specs=pl.BlockSpec(memory_space=pl.ANY),
        scratch_shapes=[pltpu.VMEM((MOE_X_SLOTS, bm, D_MODEL), F32),
                        pltpu.VMEM((MOE_Y_SLOTS, bm, D_MODEL), F32),
                        pltpu.VMEM((1, D_MODEL), F32),
                        pltpu.SemaphoreType.DMA((MOE_X_SLOTS,)),
                        pltpu.SemaphoreType.DMA((MOE_Y_SLOTS,))],
    )
    y = pl.pallas_call(
        functools.partial(_moe_kernel, bm=bm),
        grid_spec=grid_spec,
        out_shape=jax.ShapeDtypeStruct((TOP_K * slab, D_MODEL), F32),
        compiler_params=_cparams(1),
        name="moe_experts",
    )(block_expert, n_used.reshape(1), gather_idx, gather_idx, gather_idx, scatter_idx, x,
      w_gate, w_up, w_down)
    return y.reshape(TOP_K, slab, D_MODEL)


def _final_kernel(h1_ref, *rest):
    y_refs = rest[:TOP_K]
    wts_ref, g_ref, b_ref, o_ref = rest[TOP_K:]
    ffn = jnp.zeros(h1_ref.shape, F32)
    for k in range(TOP_K):
        ffn = ffn + wts_ref[:, k:k + 1] * y_refs[k][0]
    o_ref[...] = _layer_norm(ALPHA * h1_ref[...] + ffn, g_ref[...], b_ref[...])


def _final_norm(h1, y, wts, ln_g, ln_b, *, tm):
    t = h1.shape[0]

    def const(shape):
        return pl.BlockSpec(shape, lambda i: (0,) * len(shape))

    return pl.pallas_call(
        _final_kernel,
        grid=(t // tm,),
        in_specs=[pl.BlockSpec((tm, D_MODEL), lambda i: (i, 0))]
        + [pl.BlockSpec((1, tm, D_MODEL), lambda i, k=k: (k, i, 0)) for k in range(TOP_K)]
        + [pl.BlockSpec((tm, LANES), lambda i: (i, 0)),
           const((1, D_MODEL)), const((1, D_MODEL))],
        out_specs=pl.BlockSpec((tm, D_MODEL), lambda i: (i, 0)),
        out_shape=jax.ShapeDtypeStruct((t, D_MODEL), F32),
        compiler_params=_cparams(1),
        name="final_norm",
    )(h1, *([y] * TOP_K), wts, ln_g, ln_b)


def _hi_lo_bf16(w):
    hi = w.astype(BF16)
    lo = (w - hi.astype(F32)).astype(BF16)
    return jnp.concatenate([hi, lo], axis=1)


def _layer_tail(h, attn_o, ssm_o, gates, p, *, tm, moe_rows):
    merged = _merge(attn_o, ssm_o, gates, p["w_branch_attn"], p["w_branch_ssm"], tm=tm, tn=1024)
    h1, ids, wts = _ln1_router(merged, p["w_out"], h, p["ln1_g"], p["ln1_b"],
                               p["w_route"], p["b_route"], tm=tm)
    y = _moe_experts(h1, ids[:, :TOP_K], p["w_gate"], p["w_up"], p["w_down"], bm=moe_rows)
    return _final_norm(h1, y, wts, p["ln2_g"], p["ln2_b"], tm=tm)


def kernel(x_prompt, x_sample, cache_k, cache_v, page_table, state_conv, state_ssm, rel_bias, w_in, lambda_q1, lambda_k1, lambda_q2, lambda_k2, attn_subln, conv_w, conv_b, dt_bias, a_log, d_skip, ssm_norm_w, w_branch_attn, w_branch_ssm, w_out, ln1_g, ln1_b, w_route_group, b_route_group, w_route_expert, b_route_expert, w_gate, w_up, w_down, ln2_g, ln2_b):
    bsz, seq, _ = x_prompt.shape
    n_dec = x_sample.shape[0]
    assert x_sample.shape[1] == 1

    w_main = w_in.astype(BF16)
    w_dt = jnp.pad(w_main[:, OFF_DT:OFF_GATES], ((0, 0), (0, LANES - SSM_HEADS)))
    w_gates = w_main[:, OFF_GATES:]
    lam_rows = jnp.stack([lambda_q1, lambda_k1, lambda_q2, lambda_k2]).astype(F32)
    subln = attn_subln.reshape(1, V_DIM)
    pad_heads = lambda vec: jnp.pad(vec.astype(F32), (0, LANES - SSM_HEADS))
    conv_b2 = conv_b.reshape(1, CONV_DIM)
    dskip_row = jnp.repeat(d_skip.astype(F32), SSM_HEAD_DIM).reshape(1, D_INNER)
    normw_row = ssm_norm_w.reshape(1, D_INNER)
    n_route = MOE_GROUPS + N_EXPERTS
    tail = dict(
        w_branch_attn=w_branch_attn.astype(BF16), w_branch_ssm=w_branch_ssm.astype(BF16),
        w_out=w_out.astype(BF16), ln1_g=ln1_g.reshape(1, D_MODEL), ln1_b=ln1_b.reshape(1, D_MODEL),
        w_route=_hi_lo_bf16(jnp.pad(jnp.concatenate([w_route_group, w_route_expert], axis=1),
                                    ((0, 0), (0, LANES - n_route)))),
        b_route=jnp.pad(jnp.concatenate([b_route_group, b_route_expert]),
                        (0, LANES - n_route)).reshape(1, LANES),
        w_gate=w_gate.astype(BF16), w_up=w_up.astype(BF16), w_down=w_down.astype(BF16),
        ln2_g=ln2_g.reshape(1, D_MODEL), ln2_b=ln2_b.reshape(1, D_MODEL))

    t_p = bsz * seq
    xp = x_prompt.reshape(t_p, D_MODEL)
    q, k, k_bf, v, v_bf, z, xbc, dt, gates = _in_projection(
        xp.astype(BF16), w_main, w_dt, w_gates, tm=1024, q_scale=Q_SCALE * LOG2E)
    attn_p = _prompt_attention(q.reshape(bsz, seq, ATTN_WIDTH), k_bf.reshape(bsz, seq, KV_WIDTH),
                               v_bf.reshape(bsz, seq, KV_WIDTH), rel_bias, lam_rows, subln, tq=512)
    xbc3 = xbc.reshape(bsz, seq, CONV_DIM)
    ssm_p, ssm_state_p = _ssd_prompt(
        xbc3, z.reshape(bsz, seq, D_INNER), dt.reshape(bsz, seq, LANES), conv_w, conv_b2,
        pad_heads(dt_bias).reshape(LANES, 1), pad_heads(a_log).reshape(LANES, 1), dskip_row, normw_row)
    y_prompt = _layer_tail(xp, attn_p.reshape(t_p, ATTN_WIDTH), ssm_p.reshape(t_p, D_INNER), gates,
                           tail, tm=512, moe_rows=MOE_BLOCK).reshape(bsz, seq, D_MODEL)
    k_prompt = k.reshape(bsz, seq, N_KV_HEADS, V_DIM)
    v_prompt = v.reshape(bsz, seq, N_KV_HEADS, V_DIM)
    conv_prompt = xbc3[:, seq - (CONV_WIDTH - 1):, :]
    ssm_prompt = ssm_state_p.reshape(bsz, SSM_HEADS, SSM_HEAD_DIM, D_STATE)

    xs = x_sample.reshape(n_dec, D_MODEL)
    q, k, k_bf, v, v_bf, z, xbc, dt, gates = _in_projection(
        xs.astype(BF16), w_main, w_dt, w_gates, tm=n_dec, q_scale=Q_SCALE)
    attn_s = _decode_attention(q, k, v, cache_k, cache_v, page_table, rel_bias,
                               lam_rows, subln, pages_per_step=8)
    ssm_s, conv_sample, ssm_state_s = _ssd_step(
        xbc, state_conv, z, dt, state_ssm.reshape(n_dec, D_INNER, D_STATE), conv_w, conv_b2,
        pad_heads(dt_bias).reshape(1, LANES), pad_heads(a_log).reshape(1, LANES), dskip_row, normw_row)
    y_sample = _layer_tail(xs, attn_s, ssm_s, gates, tail, tm=n_dec, moe_rows=8).reshape(n_dec, 1, D_MODEL)
    k_sample = k.reshape(n_dec, 1, N_KV_HEADS, V_DIM)
    v_sample = v.reshape(n_dec, 1, N_KV_HEADS, V_DIM)
    ssm_sample = ssm_state_s.reshape(n_dec, SSM_HEADS, SSM_HEAD_DIM, D_STATE)

    return (y_prompt, y_sample, k_prompt, v_prompt, conv_prompt, ssm_prompt,
            k_sample, v_sample, conv_sample, ssm_sample)
```
